```python
import math
import numpy as np
import jax
import jax.numpy as jnp
from jax import lax


D_MODEL = 4096
BATCH = 1
SEQ = 16384
DEPTH = 2

MEM_LEN = 256
N_EVEN = (DEPTH + 1) // 2
N_ODD = DEPTH // 2
EPS = 1e-6

RET_HEADS = 8
RET_DIM = D_MODEL // 2 // RET_HEADS
RET_WIDTH = RET_HEADS * RET_DIM
RET_CHUNK = 128
ROPE_BASE = 10000.0
FOX_HEAD_DIM = 128
FOX_WIDTH = D_MODEL // 2
FOX_HEADS = FOX_WIDTH // FOX_HEAD_DIM
FOX_BLOCK = 128
EVEN_IN = 4 * RET_WIDTH + 3 * FOX_WIDTH + FOX_HEADS
EVEN_MIX = RET_WIDTH + FOX_WIDTH
SSM_INNER = 2 * D_MODEL
SSM_HEAD_DIM = 64
SSM_HEADS = SSM_INNER // SSM_HEAD_DIM
SSM_STATE = 128
SSM_GROUPS = 8
SSM_HPG = SSM_HEADS // SSM_GROUPS
SSM_CONV = 4
SSM_CHUNK = 128
SSM_CONV_DIM = SSM_INNER + 2 * SSM_GROUPS * SSM_STATE
ODD_IN = SSM_INNER + SSM_CONV_DIM + SSM_HEADS
XA_HEADS = 4
XA_HEAD_DIM = 128
XA_WIDTH = XA_HEADS * XA_HEAD_DIM
D_FF = ((8 * D_MODEL + 3 * 256 - 1) // (3 * 256)) * 256

kernel_name = "retention_fox_mamba2_hybrid"


def _normalize(x):
    xf = x.astype(jnp.float32)
    return xf * lax.rsqrt(jnp.mean(xf * xf, axis=-1, keepdims=True) + EPS)


def rms_norm(x, g):
    return (_normalize(x) * g.astype(jnp.float32)).astype(x.dtype)


def rotary(x, pos):
    half = x.shape[-1] // 2
    inv = ROPE_BASE ** (-jnp.arange(half, dtype=jnp.float32) / half)
    ang = pos.astype(jnp.float32)[:, None] * inv[None, :]
    cos = jnp.cos(ang)[None, :, None, :]
    sin = jnp.sin(ang)[None, :, None, :]
    x1, x2 = x[..., :half], x[..., half:]
    return jnp.concatenate([x1 * cos - x2 * sin, x1 * sin + x2 * cos], axis=-1)


def retention_chunkwise(q, k, v):
    b, s, h, dk = q.shape
    dv = v.shape[-1]
    L = RET_CHUNK
    n = s // L
    log_g = jnp.log1p(-jnp.exp2(-5.0 - jnp.arange(h, dtype=jnp.float32)))
    idx = jnp.arange(L, dtype=jnp.float32)
    diff = idx[:, None] - idx[None, :]
    causal = diff >= 0
    decay_in = jnp.where(causal[None], jnp.exp(log_g[:, None, None] * jnp.where(causal, diff, 0.0)[None]), 0.0)
    q_decay = jnp.exp(log_g[None, :] * (idx[:, None] + 1.0))[None, :, :, None]
    k_decay = jnp.exp(log_g[None, :] * (L - 1.0 - idx[:, None]))[None, :, :, None]
    chunk_decay = jnp.exp(log_g * L)[None, :, None, None]

    def to_chunks(t):
        return jnp.moveaxis(t.reshape(b, n, L, h, t.shape[-1]), 1, 0)

    def step(state, inp):
        qc, kc, vc = inp
        scores = jnp.einsum('bthd,bshd->bhts', qc, kc) * decay_in
        inner = jnp.einsum('bhts,bshe->bthe', scores, vc)
        cross = jnp.einsum('bthd,bhde->bthe', qc, state) * q_decay
        new_state = state * chunk_decay + jnp.einsum('bshd,bshe->bhde', kc * k_decay, vc)
        return new_state, inner + cross

    state0 = jnp.zeros((b, h, dk, dv), jnp.float32)
    _, out = lax.scan(step, state0, (to_chunks(q), to_chunks(k), to_chunks(v)))
    return jnp.moveaxis(out, 0, 1).reshape(b, s, h, dv)


def forgetting_attention(q, k, v, log_f):
    b, s, h, d = q.shape
    c = jnp.cumsum(log_f, axis=1)
    ct = jnp.transpose(c, (0, 2, 1))
    nb = s // FOX_BLOCK
    qb = jnp.moveaxis(q.reshape(b, nb, FOX_BLOCK, h, d), 1, 0)
    cb = jnp.moveaxis(c.reshape(b, nb, FOX_BLOCK, h), 1, 0)
    starts = jnp.arange(nb) * FOX_BLOCK
    kpos = jnp.arange(s)
    scale = d ** -0.5

    def block(args):
        qi, ci, st = args
        logits = jnp.einsum('bqhd,bkhd->bhqk', qi, k) * scale \
            + jnp.transpose(ci, (0, 2, 1))[..., None] - ct[:, :, None, :]
        qpos = st + jnp.arange(FOX_BLOCK)
        mask = kpos[None, :] <= qpos[:, None]
        p = jax.nn.softmax(jnp.where(mask, logits, -jnp.inf), axis=-1)
        return jnp.einsum('bhqk,bkhd->bqhd', p, v)

    out = lax.map(block, (qb, cb, starts))
    return jnp.moveaxis(out, 0, 1).reshape(b, s, h, d)


def retention_fox_mixer(h, w_in, b_f, w_out):
    b, s, _ = h.shape
    f32 = jnp.float32
    proj = h @ w_in
    sizes = [RET_WIDTH] * 4 + [FOX_WIDTH] * 3 + [FOX_HEADS]
    offs = np.cumsum(sizes)[:-1].tolist()
    rq, rk, rv, rg, fq, fk, fv, ff = jnp.split(proj, offs, axis=-1)
    pos = jnp.arange(s)
    rq = rotary(rq.astype(f32).reshape(b, s, RET_HEADS, RET_DIM), pos)
    rk = rotary(rk.astype(f32).reshape(b, s, RET_HEADS, RET_DIM), pos) * (RET_DIM ** -0.5)
    rv = rv.astype(f32).reshape(b, s, RET_HEADS, RET_DIM)
    ret = _normalize(retention_chunkwise(rq, rk, rv)).reshape(b, s, RET_WIDTH)
    ret = jax.nn.silu(rg.astype(f32)) * ret
    log_f = jax.nn.log_sigmoid(ff.astype(f32) + b_f.astype(f32))
    fox = forgetting_attention(fq.astype(f32).reshape(b, s, FOX_HEADS, FOX_HEAD_DIM),
                               fk.astype(f32).reshape(b, s, FOX_HEADS, FOX_HEAD_DIM),
                               fv.astype(f32).reshape(b, s, FOX_HEADS, FOX_HEAD_DIM),
                               log_f).reshape(b, s, FOX_WIDTH)
    mixed = jnp.concatenate([ret, fox], axis=-1).astype(h.dtype)
    return mixed @ w_out


def causal_depthwise_conv(x, w, bias):
    c = x.shape[-1]
    y = lax.conv_general_dilated(x, w[:, None, :].astype(x.dtype), window_strides=(1,),
                                 padding=[(SSM_CONV - 1, 0)],
                                 dimension_numbers=('NWC', 'WIO', 'NWC'),
                                 feature_group_count=c)
    return y + bias.astype(x.dtype)


def ssd_chunked(x, dt, a, bm, cm):
    b, s, g, e, p = x.shape
    n = bm.shape[-1]
    L = SSM_CHUNK
    nc = s // L
    xdt = x * dt[..., None]
    da = dt * a

    def chunks(t):
        return jnp.moveaxis(t.reshape((b, nc, L) + t.shape[2:]), 1, 0)

    idx = jnp.arange(L)
    causal = idx[:, None] >= idx[None, :]

    def step(state, inp):
        xc, dac, bc, cc = inp
        acs = jnp.cumsum(dac, axis=1)
        acs_t = jnp.moveaxis(acs, 1, -1)
        seg = acs_t[..., :, None] - acs_t[..., None, :]
        decay = jnp.exp(jnp.where(causal, seg, -jnp.inf))
        cb = jnp.einsum('btgn,bsgn->bgts', cc, bc)
        y_diag = jnp.einsum('bgets,bsgep->btgep', cb[:, :, None] * decay, xc)
        y_off = jnp.einsum('btgn,bgepn->btgep', cc, state) * jnp.exp(acs)[..., None]
        tail = jnp.exp(acs[:, -1:] - acs)
        new_state = state * jnp.exp(acs[:, -1])[..., None, None] \
            + jnp.einsum('bsgn,bsgep->bgepn', bc, xc * tail[..., None])
        return new_state, y_diag + y_off

    state0 = jnp.zeros((b, g, e, p, n), jnp.float32)
    _, y = lax.scan(step, state0, (chunks(xdt), chunks(da), chunks(bm), chunks(cm)))
    return jnp.moveaxis(y, 0, 1).reshape(b, s, g, e, p)


def mamba2_mixer(h, w_in, conv_w, conv_b, dt_bias, a_log, d_skip, gn_w, w_out):
    b, s, _ = h.shape
    f32 = jnp.float32
    proj = h @ w_in
    z, xbc, dt = jnp.split(proj, [SSM_INNER, SSM_INNER + SSM_CONV_DIM], axis=-1)
    xbc = jax.nn.silu(causal_depthwise_conv(xbc, conv_w, conv_b))
    xs, bm, cm = jnp.split(xbc, [SSM_INNER, SSM_INNER + SSM_GROUPS * SSM_STATE], axis=-1)
    xs = xs.astype(f32).reshape(b, s, SSM_GROUPS, SSM_HPG, SSM_HEAD_DIM)
    bm = bm.astype(f32).reshape(b, s, SSM_GROUPS, SSM_STATE)
    cm = cm.astype(f32).reshape(b, s, SSM_GROUPS, SSM_STATE)
    dt = jax.nn.softplus(dt.astype(f32) + dt_bias.astype(f32)).reshape(b, s, SSM_GROUPS, SSM_HPG)
    a = -jnp.exp(a_log.astype(f32)).reshape(SSM_GROUPS, SSM_HPG)
    y = ssd_chunked(xs, dt, a, bm, cm) + xs * d_skip.astype(f32).reshape(SSM_GROUPS, SSM_HPG)[..., None]
    y = y.reshape(b, s, SSM_INNER) * jax.nn.silu(z.astype(f32))
    y = _normalize(y.reshape(b, s, SSM_GROUPS, SSM_INNER // SSM_GROUPS)).reshape(b, s, SSM_INNER)
    y = y * gn_w.astype(f32)
    return y.astype(h.dtype) @ w_out


def memory_cross_attention(h, mem_n, wq, wk, wv, wo):
    b, s, _ = h.shape
    m = mem_n.shape[1]
    q = (h @ wq).reshape(b, s, XA_HEADS, XA_HEAD_DIM)
    k = (mem_n @ wk).reshape(b, m, XA_HEADS, XA_HEAD_DIM)
    v = (mem_n @ wv).reshape(b, m, XA_HEADS, XA_HEAD_DIM)
    logits = jnp.einsum('bqhd,bkhd->bhqk', q, k).astype(jnp.float32) * (XA_HEAD_DIM ** -0.5)
    p = jax.nn.softmax(logits, axis=-1).astype(v.dtype)
    o = jnp.einsum('bhqk,bkhd->bqhd', p, v).reshape(b, s, XA_WIDTH)
    return o @ wo


def swiglu(h, wg, wu, wd):
    return (jax.nn.silu(h @ wg) * (h @ wu)) @ wd


def setup_inputs(seed: int = 0) -> dict:
    key = jax.random.key(seed)
    ks = iter(jax.random.split(key, 48))
    f32 = jnp.float32

    def dense(shape, fan_in):
        return jax.random.normal(next(ks), shape, f32) * (fan_in ** -0.5)

    def gain(shape):
        return 1.0 + 0.02 * jax.random.normal(next(ks), shape, f32)

    def small(shape):
        return 0.01 * jax.random.normal(next(ks), shape, f32)

    x = jax.random.normal(next(ks), (BATCH, SEQ, D_MODEL), f32)
    mem = jax.random.normal(next(ks), (BATCH, MEM_LEN, D_MODEL), f32)
    mem_norm = gain((D_MODEL,))
    ev_mix_norm = gain((N_EVEN, D_MODEL))
    ev_w_in = dense((N_EVEN, D_MODEL, EVEN_IN), D_MODEL)
    ev_b_f = jax.random.uniform(next(ks), (N_EVEN, FOX_HEADS), f32, 1.0, 6.0)
    ev_w_out = dense((N_EVEN, EVEN_MIX, D_MODEL), EVEN_MIX)
    od_mix_norm = gain((N_ODD, D_MODEL))
    od_w_in = dense((N_ODD, D_MODEL, ODD_IN), D_MODEL)
    od_conv_w = dense((N_ODD, SSM_CONV, SSM_CONV_DIM), SSM_CONV)
    od_conv_b = small((N_ODD, SSM_CONV_DIM))
    u = jax.random.uniform(next(ks), (N_ODD, SSM_HEADS), f32)
    dt0 = jnp.exp(u * (math.log(0.1) - math.log(0.001)) + math.log(0.001))
    od_dt_bias = dt0 + jnp.log(-jnp.expm1(-dt0))
    od_a_log = jnp.log(jax.random.uniform(next(ks), (N_ODD, SSM_HEADS), f32, 1.0, 16.0))
    od_d_skip = gain((N_ODD, SSM_HEADS))
    od_gn_w = gain((N_ODD, SSM_INNER))
    od_w_out = dense((N_ODD, SSM_INNER, D_MODEL), SSM_INNER)
    xa_norm = gain((DEPTH, D_MODEL))
    xa_wq = dense((DEPTH, D_MODEL, XA_WIDTH), D_MODEL)
    xa_wk = dense((DEPTH, D_MODEL, XA_WIDTH), D_MODEL)
    xa_wv = dense((DEPTH, D_MODEL, XA_WIDTH), D_MODEL)
    xa_wo = dense((DEPTH, XA_WIDTH, D_MODEL), XA_WIDTH)
    ffn_norm = gain((DEPTH, D_MODEL))
    ffn_w_gate = dense((DEPTH, D_MODEL, D_FF), D_MODEL)
    ffn_w_up = dense((DEPTH, D_MODEL, D_FF), D_MODEL)
    ffn_w_down = dense((DEPTH, D_FF, D_MODEL), D_FF)
    final_norm = gain((D_MODEL,))
    return {"x": x, "mem": mem, "mem_norm": mem_norm,
            "ev_mix_norm": ev_mix_norm, "ev_w_in": ev_w_in, "ev_b_f": ev_b_f, "ev_w_out": ev_w_out,
            "od_mix_norm": od_mix_norm, "od_w_in": od_w_in, "od_conv_w": od_conv_w, "od_conv_b": od_conv_b,
            "od_dt_bias": od_dt_bias, "od_a_log": od_a_log, "od_d_skip": od_d_skip, "od_gn_w": od_gn_w,
            "od_w_out": od_w_out,
            "xa_norm": xa_norm, "xa_wq": xa_wq, "xa_wk": xa_wk, "xa_wv": xa_wv, "xa_wo": xa_wo,
            "ffn_norm": ffn_norm, "ffn_w_gate": ffn_w_gate, "ffn_w_up": ffn_w_up, "ffn_w_down": ffn_w_down,
            "final_norm": final_norm}


def reference(x, mem, mem_norm,
              ev_mix_norm, ev_w_in, ev_b_f, ev_w_out,
              od_mix_norm, od_w_in, od_conv_w, od_conv_b, od_dt_bias, od_a_log, od_d_skip, od_gn_w, od_w_out,
              xa_norm, xa_wq, xa_wk, xa_wv, xa_wo,
              ffn_norm, ffn_w_gate, ffn_w_up, ffn_w_down,
              final_norm):
    mem_n = rms_norm(mem, mem_norm)
    for layer in range(DEPTH):
        i = layer // 2
        if layer % 2 == 0:
            h = rms_norm(x, ev_mix_norm[i])
            delta = retention_fox_mixer(h, ev_w_in[i], ev_b_f[i], ev_w_out[i])
        else:
            h = rms_norm(x, od_mix_norm[i])
            delta = mamba2_mixer(h, od_w_in[i], od_conv_w[i], od_conv_b[i], od_dt_bias[i],
                                 od_a_log[i], od_d_skip[i], od_gn_w[i], od_w_out[i])
        x = x + delta.astype(x.dtype)
        h = rms_norm(x, xa_norm[layer])
        x = x + memory_cross_attention(h, mem_n, xa_wq[layer], xa_wk[layer], xa_wv[layer], xa_wo[layer]).astype(x.dtype)
        h = rms_norm(x, ffn_norm[layer])
        x = x + swiglu(h, ffn_w_gate[layer], ffn_w_up[layer], ffn_w_down[layer]).astype(x.dtype)
    return rms_norm(x, final_norm)
```

```python
import functools
import math

import jax
import jax.numpy as jnp
from jax import lax
from jax.experimental import pallas as pl
from jax.experimental.pallas import tpu as pltpu

F32 = jnp.float32
BF16 = jnp.bfloat16

D_MODEL = 4096
EPS = 1e-6
RET_HEADS = 8
RET_DIM = 256
RET_WIDTH = RET_HEADS * RET_DIM
ROPE_BASE = 10000.0
FOX_HEAD_DIM = 128
FOX_WIDTH = 2048
FOX_HEADS = FOX_WIDTH // FOX_HEAD_DIM
SSM_INNER = 8192
SSM_HEAD_DIM = 64
SSM_HEADS = SSM_INNER // SSM_HEAD_DIM
SSM_STATE = 128
SSM_GROUPS = 8
SSM_HPG = SSM_HEADS // SSM_GROUPS
SSM_GROUP_WIDTH = SSM_INNER // SSM_GROUPS
SSM_CONV = 4
SSM_CONV_DIM = SSM_INNER + 2 * SSM_GROUPS * SSM_STATE
XA_HEADS = 4
XA_HEAD_DIM = 128
XA_WIDTH = XA_HEADS * XA_HEAD_DIM
D_FF = 11008

V7X_VMEM_BYTES = 64 * 1024 * 1024
VMEM_LIMIT_BYTES = V7X_VMEM_BYTES - 8 * 1024 * 1024
LANES = 128
SUBLANES = 8

TM = 1024
D_FF_PAD = 11264
DOWN_TK = 2816
RET_CHUNK = 256
RET_BLOCK = 512
FOX_BLOCK = 512
SSD_CHUNK = 128
CUMSUM_BLOCK = 256
NEG_BIG = -1e30


def _cparams(*sem):
    return pltpu.CompilerParams(dimension_semantics=sem, vmem_limit_bytes=VMEM_LIMIT_BYTES)


def _silu(x):
    return x * (1.0 / (1.0 + jnp.exp(-x)))


def _softplus(x):
    return jnp.maximum(x, 0.0) + jnp.log1p(jnp.exp(-jnp.abs(x)))


def _dot(a, b):
    return jnp.dot(a, b, preferred_element_type=F32)


def _dot_nt(a, b):
    return lax.dot_general(a, b, (((1,), (1,)), ((), ())), preferred_element_type=F32)


def _dot_tn(a, b):
    return lax.dot_general(a, b, (((0,), (0,)), ((), ())), preferred_element_type=F32)


def _dot_f32(a, b):
    return jnp.dot(a, b, preferred_element_type=F32, precision=lax.Precision.HIGHEST)


def _lower_tri(n):
    r = lax.broadcasted_iota(jnp.int32, (n, n), 0)
    c = lax.broadcasted_iota(jnp.int32, (n, n), 1)
    return r >= c


def _rmsnorm_kernel(x_ref, g_ref, o_ref):
    x = x_ref[...]
    ms = jnp.mean(x * x, axis=-1, keepdims=True)
    o_ref[...] = (x * lax.rsqrt(ms + EPS) * g_ref[...]).astype(o_ref.dtype)


def rmsnorm(x, g, out_dtype, tm=512):
    m, d = x.shape
    tm = min(tm, m)
    return pl.pallas_call(
        _rmsnorm_kernel,
        grid=(m // tm,),
        in_specs=[pl.BlockSpec((tm, d), lambda i: (i, 0)),
                  pl.BlockSpec((1, d), lambda i: (0, 0))],
        out_specs=pl.BlockSpec((tm, d), lambda i: (i, 0)),
        out_shape=jax.ShapeDtypeStruct((m, d), out_dtype),
        compiler_params=_cparams("parallel"),
        name="rmsnorm",
    )(x, g.reshape(1, d).astype(F32))


def _mm_kernel(*refs, n_a, n_extra, epilogue):
    a_refs = refs[:n_a]
    w_ref = refs[n_a]
    extra_refs = refs[n_a + 1:n_a + 1 + n_extra]
    out_refs = refs[n_a + 1 + n_extra:]
    acc = None
    off = 0
    for a_ref in a_refs:
        k = a_ref.shape[1]
        part = _dot(a_ref[...], w_ref[off:off + k, :])
        acc = part if acc is None else acc + part
        off += k
    epilogue(acc, extra_refs, out_refs)


def matmul(a_list, w, *, col_off, n_cols, tn, epilogue, extras=(), outs, tm=TM, name):
    m = a_list[0].shape[0]
    tm = min(tm, m)
    k_total = w.shape[0]
    assert sum(a.shape[1] for a in a_list) == k_total
    assert col_off % tn == 0 and n_cols % tn == 0 and m % tm == 0
    off_blocks = col_off // tn
    in_specs = [pl.BlockSpec((tm, a.shape[1]), lambda i, j: (i, 0)) for a in a_list]
    in_specs.append(pl.BlockSpec((k_total, tn), lambda i, j: (0, j + off_blocks)))
    in_specs += [pl.BlockSpec(bs, im) for (_, bs, im) in extras]
    out_specs = [pl.BlockSpec(bs, im) for (_, _, bs, im) in outs]
    out_shape = [jax.ShapeDtypeStruct(s, dt) for (s, dt, _, _) in outs]
    res = pl.pallas_call(
        functools.partial(_mm_kernel, n_a=len(a_list), n_extra=len(extras), epilogue=epilogue),
        grid=(m // tm, n_cols // tn),
        in_specs=in_specs,
        out_specs=out_specs,
        out_shape=out_shape,
        compiler_params=_cparams("parallel", "arbitrary"),
        name=name,
    )(*a_list, w, *[e[0] for e in extras])
    return res


def _epi_scale(acc, extra_refs, out_refs, *, scale):
    out_refs[0][...] = (acc * scale if scale != 1.0 else acc).astype(out_refs[0].dtype)


def _epi_silu(acc, extra_refs, out_refs):
    out_refs[0][...] = _silu(acc).astype(out_refs[0].dtype)


def _epi_residual(acc, extra_refs, out_refs):
    out_refs[0][...] = extra_refs[0][...] + acc


def _epi_rotary(acc, extra_refs, out_refs):
    cos = extra_refs[0][...]
    sin = extra_refs[1][...]
    half = RET_DIM // 2
    scale = jnp.where(pl.program_id(1) < RET_HEADS, 1.0, RET_DIM ** -0.5).astype(F32)
    x1 = acc[:, :half]
    x2 = acc[:, half:]
    out_refs[0][:, :half] = ((x1 * cos - x2 * sin) * scale).astype(BF16)
    out_refs[0][:, half:] = ((x1 * sin + x2 * cos) * scale).astype(BF16)


def _epi_log_sigmoid(acc, extra_refs, out_refs):
    z = acc + extra_refs[0][...]
    out_refs[0][...] = jnp.minimum(z, 0.0) - jnp.log1p(jnp.exp(-jnp.abs(z)))


def _epi_softplus(acc, extra_refs, out_refs):
    out_refs[0][...] = _softplus(acc + extra_refs[0][...])


def _simple_out(m, n, tm, tn, dtype):
    tm = min(tm, m)
    return [((m, n), dtype, (tm, tn), lambda i, j: (i, j))]


def _gateup_kernel(h_ref, wg_ref, wu_ref, o_ref):
    h = h_ref[...]
    g = _dot(h, wg_ref[...])
    u = _dot(h, wu_ref[...])
    o_ref[...] = (_silu(g) * u).astype(o_ref.dtype)


def gateup(h, wg, wu, tn=512):
    m, k = h.shape
    n = wg.shape[1]
    tm = min(TM, m)
    return pl.pallas_call(
        _gateup_kernel,
        grid=(m // tm, n // tn),
        in_specs=[pl.BlockSpec((tm, k), lambda i, j: (i, 0)),
                  pl.BlockSpec((k, tn), lambda i, j: (0, j)),
                  pl.BlockSpec((k, tn), lambda i, j: (0, j))],
        out_specs=pl.BlockSpec((tm, tn), lambda i, j: (i, j)),
        out_shape=jax.ShapeDtypeStruct((m, n), BF16),
        compiler_params=_cparams("parallel", "arbitrary"),
        name="ffn_gateup",
    )(h, wg, wu)


def _mm_k_residual_kernel(a_ref, w_ref, x_ref, o_ref, acc_ref):
    kk = pl.program_id(2)
    part = _dot(a_ref[...], w_ref[...])

    @pl.when(kk == 0)
    def _():
        acc_ref[...] = part

    @pl.when(kk > 0)
    def _():
        acc_ref[...] += part

    @pl.when(kk == pl.num_programs(2) - 1)
    def _():
        o_ref[...] = x_ref[...] + acc_ref[...]


def matmul_k_residual(a, w, x, *, tk, tn=1024, name):
    m, k = a.shape
    n = w.shape[1]
    tm = min(TM, m)
    assert k % tk == 0 and n % tn == 0
    return pl.pallas_call(
        _mm_k_residual_kernel,
        grid=(m // tm, n // tn, k // tk),
        in_specs=[pl.BlockSpec((tm, tk), lambda i, j, kk: (i, kk)),
                  pl.BlockSpec((tk, tn), lambda i, j, kk: (kk, j)),
                  pl.BlockSpec((tm, tn), lambda i, j, kk: (i, j))],
        out_specs=pl.BlockSpec((tm, tn), lambda i, j, kk: (i, j)),
        out_shape=jax.ShapeDtypeStruct((m, n), F32),
        scratch_shapes=[pltpu.VMEM((tm, tn), F32)],
        compiler_params=_cparams("parallel", "parallel", "arbitrary"),
        name=name,
    )(a, w, x)


def _cumsum_kernel(x_ref, o_ref, carry_ref):
    @pl.when(pl.program_id(0) == 0)
    def _():
        carry_ref[...] = jnp.zeros_like(carry_ref)

    n = x_ref.shape[0]
    tri = _lower_tri(n).astype(F32)
    c = _dot_f32(tri, x_ref[...]) + carry_ref[0:1, :]
    o_ref[...] = c
    carry_ref[...] = jnp.broadcast_to(c[n - 1:n, :], carry_ref.shape)


def cumsum_rows(x):
    m, n = x.shape
    tb = min(CUMSUM_BLOCK, m)
    return pl.pallas_call(
        _cumsum_kernel,
        grid=(m // tb,),
        in_specs=[pl.BlockSpec((tb, n), lambda i: (i, 0))],
        out_specs=pl.BlockSpec((tb, n), lambda i: (i, 0)),
        out_shape=jax.ShapeDtypeStruct((m, n), F32),
        scratch_shapes=[pltpu.VMEM((SUBLANES, n), F32)],
        compiler_params=_cparams("arbitrary"),
        name="cumsum_rows",
    )(x)


def _retention_kernel(lg_ref, q_ref, k_ref, v_ref, g_ref, o_ref, state_ref, *, chunk, n_sub):
    head = pl.program_id(0)

    @pl.when(pl.program_id(1) == 0)
    def _():
        state_ref[...] = jnp.zeros_like(state_ref)

    lg = lg_ref[head]
    L = chunk
    ti = lax.broadcasted_iota(jnp.int32, (L, L), 0)
    si = lax.broadcasted_iota(jnp.int32, (L, L), 1)
    causal = ti >= si
    diff = jnp.where(causal, (ti - si).astype(F32), 0.0)
    decay_in = jnp.where(causal, jnp.exp(lg * diff), 0.0)
    idx = lax.broadcasted_iota(jnp.int32, (L, 1), 0).astype(F32)
    q_decay = jnp.exp(lg * (idx + 1.0))
    k_decay = jnp.exp(lg * (L - 1.0 - idx))
    chunk_decay = jnp.exp(lg * float(L))

    for u in range(n_sub):
        rows = slice(u * L, (u + 1) * L)
        q = q_ref[rows, :]
        k = k_ref[rows, :]
        v = v_ref[rows, :]
        state = state_ref[...]
        scores = _dot_nt(q, k) * decay_in
        inner = _dot(scores.astype(BF16), v)
        cross = _dot(q, state.astype(BF16)) * q_decay
        out = inner + cross
        kd = (k.astype(F32) * k_decay).astype(BF16)
        state_ref[...] = state * chunk_decay + _dot_tn(kd, v)
        ms = jnp.mean(out * out, axis=-1, keepdims=True)
        o_ref[rows, :] = (out * lax.rsqrt(ms + EPS) * g_ref[rows, :].astype(F32)).astype(o_ref.dtype)


def retention(qk_rot, v, gate, log_gamma):
    s = v.shape[0]
    chunk = min(RET_CHUNK, s)
    tb = min(RET_BLOCK, s)
    return pl.pallas_call(
        functools.partial(_retention_kernel, chunk=chunk, n_sub=tb // chunk),
        grid=(RET_HEADS, s // tb),
        in_specs=[pl.BlockSpec(memory_space=pltpu.SMEM),
                  pl.BlockSpec((tb, RET_DIM), lambda h, c: (c, h)),
                  pl.BlockSpec((tb, RET_DIM), lambda h, c: (c, RET_HEADS + h)),
                  pl.BlockSpec((tb, RET_DIM), lambda h, c: (c, h)),
                  pl.BlockSpec((tb, RET_DIM), lambda h, c: (c, h))],
        out_specs=pl.BlockSpec((tb, RET_DIM), lambda h, c: (c, h)),
        out_shape=jax.ShapeDtypeStruct((s, RET_WIDTH), BF16),
        scratch_shapes=[pltpu.VMEM((RET_DIM, RET_DIM), F32)],
        compiler_params=_cparams("parallel", "arbitrary"),
        name="retention",
    )(log_gamma, qk_rot, qk_rot, v, gate)


def _fox_kernel(q_ref, k_ref, v_ref, c_ref, ct_ref, o_ref, m_ref, l_ref, acc_ref, *, blk):
    head = pl.program_id(0)
    qi = pl.program_id(1)
    q = q_ref[...]
    lane = lax.broadcasted_iota(jnp.int32, c_ref.shape, 1)
    c_t = jnp.sum(jnp.where(lane == head, c_ref[...], 0.0), axis=1, keepdims=True)

    m_ref[...] = jnp.full_like(m_ref, NEG_BIG)
    l_ref[...] = jnp.zeros_like(l_ref)
    acc_ref[...] = jnp.zeros_like(acc_ref)

    def step(j, masked):
        start = pl.multiple_of(j * blk, blk)
        k = k_ref[pl.ds(start, blk), :]
        v = v_ref[pl.ds(start, blk), :]
        s = _dot_nt(q, k) + (c_t - ct_ref[j])
        if masked:
            s = jnp.where(_lower_tri(blk), s, NEG_BIG)
        m_old = m_ref[...]
        m_new = jnp.maximum(m_old, jnp.max(s, axis=1, keepdims=True))
        alpha = jnp.exp(m_old - m_new)
        p = jnp.exp(s - m_new)
        l_ref[...] = alpha * l_ref[...] + jnp.sum(p, axis=1, keepdims=True)
        acc_ref[...] = alpha * acc_ref[...] + _dot(p.astype(BF16), v)
        m_ref[...] = m_new

    def body(j, carry):
        step(j, False)
        return carry

    lax.fori_loop(0, qi, body, 0)
    step(qi, True)
    o_ref[...] = (acc_ref[...] / l_ref[...]).astype(o_ref.dtype)


def fox_attention(q, kv, c, c_t):
    s = q.shape[0]
    blk = min(FOX_BLOCK, s)
    d = FOX_HEAD_DIM
    return pl.pallas_call(
        functools.partial(_fox_kernel, blk=blk),
        grid=(FOX_HEADS, s // blk),
        in_specs=[pl.BlockSpec((blk, d), lambda h, i: (i, h)),
                  pl.BlockSpec((s, d), lambda h, i: (0, h)),
                  pl.BlockSpec((s, d), lambda h, i: (0, FOX_HEADS + h)),
                  pl.BlockSpec((blk, LANES), lambda h, i: (i, 0)),
                  pl.BlockSpec((None, s // blk, 1, blk), lambda h, i: (h, 0, 0, 0))],
        out_specs=pl.BlockSpec((blk, d), lambda h, i: (i, h)),
        out_shape=jax.ShapeDtypeStruct((s, FOX_WIDTH), BF16),
        scratch_shapes=[pltpu.VMEM((blk, 1), F32), pltpu.VMEM((blk, 1), F32), pltpu.VMEM((blk, d), F32)],
        compiler_params=_cparams("parallel", "arbitrary"),
        name="fox_attention",
    )(q, kv, kv, c, c_t.reshape(FOX_HEADS, s // blk, 1, blk))


def _xattn_kernel(h_ref, wq_ref, kt_ref, v_ref, wo_ref, x_ref, o_ref):
    q = _dot(h_ref[...], wq_ref[...]).astype(BF16)
    heads = []
    for hd in range(XA_HEADS):
        cols = slice(hd * XA_HEAD_DIM, (hd + 1) * XA_HEAD_DIM)
        logits = _dot(q[:, cols], kt_ref[hd]) * (XA_HEAD_DIM ** -0.5)
        mx = jnp.max(logits, axis=-1, keepdims=True)
        e = jnp.exp(logits - mx)
        p = e / jnp.sum(e, axis=-1, keepdims=True)
        heads.append(_dot(p.astype(BF16), v_ref[:, cols]))
    o = jnp.concatenate(heads, axis=-1).astype(BF16)
    o_ref[...] = x_ref[...] + _dot(o, wo_ref[...])


def cross_attention(h, wq, k_t, v, wo, x, tm=256):
    s, d = h.shape
    tm = min(tm, s)
    mem_len = v.shape[0]
    const2 = lambda i: (0, 0)
    return pl.pallas_call(
        _xattn_kernel,
        grid=(s // tm,),
        in_specs=[pl.BlockSpec((tm, d), lambda i: (i, 0)),
                  pl.BlockSpec((d, XA_WIDTH), const2),
                  pl.BlockSpec((XA_HEADS, XA_HEAD_DIM, mem_len), lambda i: (0, 0, 0)),
                  pl.BlockSpec((mem_len, XA_WIDTH), const2),
                  pl.BlockSpec((XA_WIDTH, d), const2),
                  pl.BlockSpec((tm, d), lambda i: (i, 0))],
        out_specs=pl.BlockSpec((tm, d), lambda i: (i, 0)),
        out_shape=jax.ShapeDtypeStruct((s, d), F32),
        compiler_params=_cparams("parallel"),
        name="cross_attention",
    )(h, wq, k_t, v, wo, x)


def _conv_silu(buf_ref, blk, w, b, L):
    buf_ref[SUBLANES:SUBLANES + L, :] = blk
    y = b
    for kk in range(SSM_CONV):
        lo = SUBLANES - (SSM_CONV - 1) + kk
        y = y + w[kk:kk + 1, :] * buf_ref[lo:lo + L, :]
    buf_ref[0:SUBLANES, :] = blk[L - SUBLANES:, :]
    return _silu(y)


def _ssd_kernel(x_ref, b_ref, c_ref, wx_ref, wb_ref, wc_ref, bx_ref, bb_ref, bc_ref,
                dt_ref, dtt_ref, a_ref, at_ref, dskip_ref, z_ref, gnw_ref,
                y_ref, state_ref, xbuf_ref, bbuf_ref, cbuf_ref, *, L):
    @pl.when(pl.program_id(1) == 0)
    def _():
        state_ref[...] = jnp.zeros_like(state_ref)
        xbuf_ref[...] = jnp.zeros_like(xbuf_ref)
        bbuf_ref[...] = jnp.zeros_like(bbuf_ref)
        cbuf_ref[...] = jnp.zeros_like(cbuf_ref)

    xs = _conv_silu(xbuf_ref, x_ref[...].astype(F32), wx_ref[...], bx_ref[...], L)
    bm = _conv_silu(bbuf_ref, b_ref[...].astype(F32), wb_ref[...], bb_ref[...], L)
    cm = _conv_silu(cbuf_ref, c_ref[...].astype(F32), wc_ref[...], bc_ref[...], L)
    bm16 = bm.astype(BF16)
    cm16 = cm.astype(BF16)

    causal = _lower_tri(L)
    tri = causal.astype(F32)
    dt = dt_ref[...]
    da = dt * a_ref[...]
    acs = _dot_f32(tri, da)
    acs_t = lax.dot_general(dtt_ref[...] * at_ref[...], tri, (((1,), (1,)), ((), ())),
                            preferred_element_type=F32, precision=lax.Precision.HIGHEST)
    acs_last = acs[L - 1:L, :]
    exp_acs = jnp.exp(acs)
    tail = jnp.exp(acs_last - acs)
    exp_last = jnp.exp(acs_last)

    cb = _dot_nt(cm16, bm16)
    state = state_ref[...]
    y_off = _dot(cm16, state.astype(BF16))

    lane = lax.broadcasted_iota(jnp.int32, (L, LANES), 1)
    first_half = lane < SSM_HEAD_DIM
    lane_row = lax.broadcasted_iota(jnp.int32, (1, LANES), 1) < SSM_HEAD_DIM
    y_parts = []
    xt_parts = []
    sdec_parts = []
    for pr in range(SSM_HPG // 2):
        e0, e1 = 2 * pr, 2 * pr + 1
        cols = slice(pr * LANES, (pr + 1) * LANES)

        def pair(col_arr):
            return jnp.where(first_half,
                             jnp.broadcast_to(col_arr[:, e0:e0 + 1], (L, LANES)),
                             jnp.broadcast_to(col_arr[:, e1:e1 + 1], (L, LANES)))

        x_pair = xs[:, cols]
        xdt = x_pair * pair(dt)
        m0 = jnp.exp(jnp.where(causal, acs[:, e0:e0 + 1] - acs_t[e0:e0 + 1, :], NEG_BIG)) * cb
        m1 = jnp.exp(jnp.where(causal, acs[:, e1:e1 + 1] - acs_t[e1:e1 + 1, :], NEG_BIG)) * cb
        mcat = jnp.concatenate([m0, m1], axis=1).astype(BF16)
        xdt16 = xdt.astype(BF16)
        zero = jnp.zeros_like(xdt16)
        bd = jnp.concatenate([jnp.where(first_half, xdt16, zero),
                              jnp.where(first_half, zero, xdt16)], axis=0)
        y_diag = _dot(mcat, bd)
        y_parts.append(y_diag + y_off[:, cols] * pair(exp_acs))
        xt_parts.append((xdt * pair(tail)).astype(BF16))
        sdec_parts.append(jnp.where(lane_row,
                                    jnp.broadcast_to(exp_last[:, e0:e0 + 1], (1, LANES)),
                                    jnp.broadcast_to(exp_last[:, e1:e1 + 1], (1, LANES))))
    y = jnp.concatenate(y_parts, axis=1)
    xt = jnp.concatenate(xt_parts, axis=1)
    sdec = jnp.concatenate(sdec_parts, axis=1)
    state_ref[...] = state * sdec + _dot_tn(bm16, xt)

    y = (y + xs * dskip_ref[...]) * z_ref[...].astype(F32)
    ms = jnp.mean(y * y, axis=-1, keepdims=True)
    y_ref[...] = (y * lax.rsqrt(ms + EPS) * gnw_ref[...]).astype(y_ref.dtype)


def ssd_mixer(xbc, conv_w, conv_b, dt_g, dt_gt, a_g, a_gt, dskip_row, z_silu, gn_w):
    s = xbc.shape[0]
    L = min(SSD_CHUNK, s)
    gw = SSM_GROUP_WIDTH
    n = SSM_STATE
    xb = SSM_INNER // n
    cbk = xb + SSM_GROUPS
    return pl.pallas_call(
        functools.partial(_ssd_kernel, L=L),
        grid=(SSM_GROUPS, s // L),
        in_specs=[pl.BlockSpec((L, gw), lambda g, c: (c, g)),
                  pl.BlockSpec((L, n), lambda g, c: (c, xb + g)),
                  pl.BlockSpec((L, n), lambda g, c: (c, cbk + g)),
                  pl.BlockSpec((SSM_CONV, gw), lambda g, c: (0, g)),
                  pl.BlockSpec((SSM_CONV, n), lambda g, c: (0, xb + g)),
                  pl.BlockSpec((SSM_CONV, n), lambda g, c: (0, cbk + g)),
                  pl.BlockSpec((1, gw), lambda g, c: (0, g)),
                  pl.BlockSpec((1, n), lambda g, c: (0, xb + g)),
                  pl.BlockSpec((1, n), lambda g, c: (0, cbk + g)),
                  pl.BlockSpec((None, L, SSM_HPG), lambda g, c: (g, c, 0)),
                  pl.BlockSpec((None, SSM_HPG, L), lambda g, c: (g, 0, c)),
                  pl.BlockSpec((None, 1, SSM_HPG), lambda g, c: (g, 0, 0)),
                  pl.BlockSpec((None, SSM_HPG, 1), lambda g, c: (g, 0, 0)),
                  pl.BlockSpec((1, gw), lambda g, c: (0, g)),
                  pl.BlockSpec((L, gw), lambda g, c: (c, g)),
                  pl.BlockSpec((1, gw), lambda g, c: (0, g))],
        out_specs=pl.BlockSpec((L, gw), lambda g, c: (c, g)),
        out_shape=jax.ShapeDtypeStruct((s, SSM_INNER), BF16),
        scratch_shapes=[pltpu.VMEM((n, gw), F32),
                        pltpu.VMEM((L + SUBLANES, gw), F32),
                        pltpu.VMEM((L + SUBLANES, n), F32),
                        pltpu.VMEM((L + SUBLANES, n), F32)],
        compiler_params=_cparams("parallel", "arbitrary"),
        name="ssd_mixer",
    )(xbc, xbc, xbc, conv_w, conv_w, conv_w, conv_b, conv_b, conv_b,
      dt_g, dt_gt, a_g, a_gt, dskip_row, z_silu, gn_w)


def _retention_fox_layer(x, norm_g, w_in, b_f, w_out):
    s = x.shape[0]
    tm = min(TM, s)
    h = rmsnorm(x, norm_g, BF16)
    w_in16 = w_in.astype(BF16)

    half = RET_DIM // 2
    inv = ROPE_BASE ** (-jnp.arange(half, dtype=F32) / half)
    ang = jnp.arange(s).astype(F32)[:, None] * inv[None, :]
    cos, sin = jnp.cos(ang), jnp.sin(ang)

    (qk_rot,) = matmul([h], w_in16, col_off=0, n_cols=2 * RET_WIDTH, tn=RET_DIM, epilogue=_epi_rotary,
                       extras=[(cos, (tm, half), lambda i, j: (i, 0)), (sin, (tm, half), lambda i, j: (i, 0))],
                       outs=_simple_out(s, 2 * RET_WIDTH, tm, RET_DIM, BF16), name="ev_proj_rotary")
    (rv,) = matmul([h], w_in16, col_off=2 * RET_WIDTH, n_cols=RET_WIDTH, tn=512,
                   epilogue=functools.partial(_epi_scale, scale=1.0),
                   outs=_simple_out(s, RET_WIDTH, tm, 512, BF16), name="ev_proj_rv")
    (rg,) = matmul([h], w_in16, col_off=3 * RET_WIDTH, n_cols=RET_WIDTH, tn=512, epilogue=_epi_silu,
                   outs=_simple_out(s, RET_WIDTH, tm, 512, BF16), name="ev_proj_gate")
    (fq,) = matmul([h], w_in16, col_off=4 * RET_WIDTH, n_cols=FOX_WIDTH, tn=512,
                   epilogue=functools.partial(_epi_scale, scale=FOX_HEAD_DIM ** -0.5),
                   outs=_simple_out(s, FOX_WIDTH, tm, 512, BF16), name="ev_proj_fq")
    (fkv,) = matmul([h], w_in16, col_off=4 * RET_WIDTH + FOX_WIDTH, n_cols=2 * FOX_WIDTH, tn=512,
                    epilogue=functools.partial(_epi_scale, scale=1.0),
                    outs=_simple_out(s, 2 * FOX_WIDTH, tm, 512, BF16), name="ev_proj_fkv")

    ff_off = 4 * RET_WIDTH + 3 * FOX_WIDTH
    w_ff = jnp.pad(w_in16[:, ff_off:ff_off + FOX_HEADS], ((0, 0), (0, LANES - FOX_HEADS)))
    b_pad = jnp.pad(b_f.astype(F32), (0, LANES - FOX_HEADS)).reshape(1, LANES)
    (log_f,) = matmul([h], w_ff, col_off=0, n_cols=LANES, tn=LANES, epilogue=_epi_log_sigmoid,
                      extras=[(b_pad, (1, LANES), lambda i, j: (0, 0))],
                      outs=_simple_out(s, LANES, tm, LANES, F32), name="ev_proj_forget")
    c = cumsum_rows(log_f)
    c_t = c[:, :FOX_HEADS].T

    log_gamma = jnp.log1p(-jnp.exp2(-5.0 - jnp.arange(RET_HEADS, dtype=F32)))
    ret = retention(qk_rot, rv, rg, log_gamma)
    fox = fox_attention(fq, fkv, c, c_t)

    (x_new,) = matmul([ret, fox], w_out.astype(BF16), col_off=0, n_cols=D_MODEL, tn=1024, epilogue=_epi_residual,
                      extras=[(x, (tm, 1024), lambda i, j: (i, j))],
                      outs=_simple_out(s, D_MODEL, tm, 1024, F32), name="ev_out_proj")
    return x_new


def _mamba2_layer(x, norm_g, w_in, conv_w, conv_b, dt_bias, a_log, d_skip, gn_w, w_out):
    s = x.shape[0]
    tm = min(TM, s)
    h = rmsnorm(x, norm_g, BF16)
    w_in16 = w_in.astype(BF16)
    (z_silu,) = matmul([h], w_in16, col_off=0, n_cols=SSM_INNER, tn=512, epilogue=_epi_silu,
                       outs=_simple_out(s, SSM_INNER, tm, 512, BF16), name="od_proj_z")
    (xbc,) = matmul([h], w_in16, col_off=SSM_INNER, n_cols=SSM_CONV_DIM, tn=512,
                    epilogue=functools.partial(_epi_scale, scale=1.0),
                    outs=_simple_out(s, SSM_CONV_DIM, tm, 512, BF16), name="od_proj_xbc")
    (dt,) = matmul([h], w_in16, col_off=SSM_INNER + SSM_CONV_DIM, n_cols=SSM_HEADS, tn=SSM_HEADS,
                   epilogue=_epi_softplus,
                   extras=[(dt_bias.astype(F32).reshape(1, SSM_HEADS), (1, SSM_HEADS), lambda i, j: (0, 0))],
                   outs=_simple_out(s, SSM_HEADS, tm, SSM_HEADS, F32), name="od_proj_dt")

    dt_g = dt.reshape(s, SSM_GROUPS, SSM_HPG).transpose(1, 0, 2)
    dt_gt = dt_g.transpose(0, 2, 1)
    a = -jnp.exp(a_log.astype(F32))
    a_g = a.reshape(SSM_GROUPS, 1, SSM_HPG)
    a_gt = a.reshape(SSM_GROUPS, SSM_HPG, 1)
    dskip_row = jnp.repeat(d_skip.astype(F32), SSM_HEAD_DIM).reshape(1, SSM_INNER)
    y = ssd_mixer(xbc, conv_w.astype(F32), conv_b.astype(F32).reshape(1, SSM_CONV_DIM),
                  dt_g, dt_gt, a_g, a_gt, dskip_row, z_silu, gn_w.astype(F32).reshape(1, SSM_INNER))
    return matmul_k_residual(y, w_out.astype(BF16), x, tk=2048, name="od_out_proj")


def _cross_attention_layer(x, norm_g, mem_n, wq, wk, wv, wo):
    h = rmsnorm(x, norm_g, BF16)
    m = mem_n.shape[0]
    (k,) = matmul([mem_n], wk.astype(BF16), col_off=0, n_cols=XA_WIDTH, tn=XA_WIDTH,
                  epilogue=functools.partial(_epi_scale, scale=1.0),
                  outs=_simple_out(m, XA_WIDTH, m, XA_WIDTH, BF16), name="xa_proj_k")
    (v,) = matmul([mem_n], wv.astype(BF16), col_off=0, n_cols=XA_WIDTH, tn=XA_WIDTH,
                  epilogue=functools.partial(_epi_scale, scale=1.0),
                  outs=_simple_out(m, XA_WIDTH, m, XA_WIDTH, BF16), name="xa_proj_v")
    k_t = k.reshape(m, XA_HEADS, XA_HEAD_DIM).transpose(1, 2, 0)
    return cross_attention(h, wq.astype(BF16), k_t, v, wo.astype(BF16), x)


def _ffn_layer(x, norm_g, wg, wu, wd):
    h = rmsnorm(x, norm_g, BF16)
    pad = D_FF_PAD - D_FF
    wg16 = jnp.pad(wg.astype(BF16), ((0, 0), (0, pad)))
    wu16 = jnp.pad(wu.astype(BF16), ((0, 0), (0, pad)))
    wd16 = jnp.pad(wd.astype(BF16), ((0, pad), (0, 0)))
    a = gateup(h, wg16, wu16)
    return matmul_k_residual(a, wd16, x, tk=DOWN_TK, name="ffn_down")


def kernel(x, mem, mem_norm, ev_mix_norm, ev_w_in, ev_b_f, ev_w_out, od_mix_norm, od_w_in, od_conv_w, od_conv_b,
           od_dt_bias, od_a_log, od_d_skip, od_gn_w, od_w_out, xa_norm, xa_wq, xa_wk, xa_wv, xa_wo, ffn_norm,
           ffn_w_gate, ffn_w_up, ffn_w_down, final_norm):
    b, s, d = x.shape
    assert b == 1 and d == D_MODEL
    xs = x.reshape(s, d).astype(F32)
    mem_n = rmsnorm(mem.reshape(mem.shape[1], d).astype(F32), mem_norm, BF16)
    depth = xa_norm.shape[0]
    for layer in range(depth):
        i = layer // 2
        if layer % 2 == 0:
            xs = _retention_fox_layer(xs, ev_mix_norm[i], ev_w_in[i], ev_b_f[i], ev_w_out[i])
        else:
            xs = _mamba2_layer(xs, od_mix_norm[i], od_w_in[i], od_conv_w[i], od_conv_b[i], od_dt_bias[i],
                               od_a_log[i], od_d_skip[i], od_gn_w[i], od_w_out[i])
        xs = _cross_attention_layer(xs, xa_norm[layer], mem_n, xa_wq[layer], xa_wk[layer], xa_wv[layer],
                                    xa_wo[layer])
        xs = _ffn_layer(xs, ffn_norm[layer], ffn_w_gate[layer], ffn_w_up[layer], ffn_w_down[layer])
    out = rmsnorm(xs, final_norm, F32)
    return out.reshape(b, s, d)
```

```python
import functools
import math

import jax
import jax.numpy as jnp
from jax import lax
from jax.experimental import pallas as pl
from jax.experimental.pallas import tpu as pltpu

F32 = jnp.float32
BF16 = jnp.bfloat16

D_MODEL = 4096
EPS = 1e-6
RET_HEADS = 8
RET_DIM = 256
RET_WIDTH = RET_HEADS * RET_DIM
ROPE_BASE = 10000.0
FOX_HEAD_DIM = 128
FOX_WIDTH = 2048
FOX_HEADS = FOX_WIDTH // FOX_HEAD_DIM
SSM_INNER = 8192
SSM_HEAD_DIM = 64
SSM_HEADS = SSM_INNER // SSM_HEAD_DIM
SSM_STATE = 128
SSM_GROUPS = 8
SSM_HPG = SSM_HEADS // SSM_GROUPS
SSM_GROUP_WIDTH = SSM_INNER // SSM_GROUPS
SSM_CONV = 4
SSM_CONV_DIM = SSM_INNER + 2 * SSM_GROUPS * SSM_STATE
XA_HEADS = 4
XA_HEAD_DIM = 128
XA_WIDTH = XA_HEADS * XA_HEAD_DIM
D_FF = 11008

V7X_VMEM_BYTES = 64 * 1024 * 1024
VMEM_LIMIT_BYTES = V7X_VMEM_BYTES - 8 * 1024 * 1024
LANES = 128
SUBLANES = 8

TM = 1024
D_FF_PAD = 11264
DOWN_TK = 2816
RET_CHUNK = 256
RET_BLOCK = 512
FOX_BLOCK = 512
SSD_CHUNK = 128
CUMSUM_BLOCK = 256
NEG_BIG = -1e30
LOG2E = 1.4426950408889634


def _cparams(*sem):
    return pltpu.CompilerParams(dimension_semantics=sem, vmem_limit_bytes=VMEM_LIMIT_BYTES)


def _silu(x):
    return x * (1.0 / (1.0 + jnp.exp(-x)))


def _softplus(x):
    return jnp.maximum(x, 0.0) + jnp.log1p(jnp.exp(-jnp.abs(x)))


def _dot(a, b):
    return jnp.dot(a, b, preferred_element_type=F32)


def _dot_nt(a, b):
    return lax.dot_general(a, b, (((1,), (1,)), ((), ())), preferred_element_type=F32)


def _dot_tn(a, b):
    return lax.dot_general(a, b, (((0,), (0,)), ((), ())), preferred_element_type=F32)


def _dot_f32(a, b):
    return jnp.dot(a, b, preferred_element_type=F32, precision=lax.Precision.HIGHEST)


def _lower_tri(n):
    r = lax.broadcasted_iota(jnp.int32, (n, n), 0)
    c = lax.broadcasted_iota(jnp.int32, (n, n), 1)
    return r >= c


def _rmsnorm_kernel(x_ref, g_ref, o_ref):
    x = x_ref[...]
    ms = jnp.mean(x * x, axis=-1, keepdims=True)
    o_ref[...] = (x * lax.rsqrt(ms + EPS) * g_ref[...]).astype(o_ref.dtype)


def rmsnorm(x, g, out_dtype, tm=512):
    m, d = x.shape
    tm = min(tm, m)
    return pl.pallas_call(
        _rmsnorm_kernel,
        grid=(m // tm,),
        in_specs=[pl.BlockSpec((tm, d), lambda i: (i, 0)),
                  pl.BlockSpec((1, d), lambda i: (0, 0))],
        out_specs=pl.BlockSpec((tm, d), lambda i: (i, 0)),
        out_shape=jax.ShapeDtypeStruct((m, d), out_dtype),
        compiler_params=_cparams("parallel"),
        name="rmsnorm",
    )(x, g.reshape(1, d).astype(F32))


def _mm_kernel(*refs, n_a, n_extra, epilogue):
    a_refs = refs[:n_a]
    w_ref = refs[n_a]
    extra_refs = refs[n_a + 1:n_a + 1 + n_extra]
    out_refs = refs[n_a + 1 + n_extra:]
    acc = None
    off = 0
    for a_ref in a_refs:
        k = a_ref.shape[1]
        part = _dot(a_ref[...], w_ref[off:off + k, :])
        acc = part if acc is None else acc + part
        off += k
    epilogue(acc, extra_refs, out_refs)


def _w_spec(w, layer, block, index_map):
    if w.ndim == 2:
        return pl.BlockSpec(block, index_map)
    return pl.BlockSpec((None,) + block, lambda *idx: (layer,) + index_map(*idx))


def matmul(a_list, w, *, col_off, n_cols, tn, epilogue, extras=(), outs, tm=TM, name, layer=0):
    m = a_list[0].shape[0]
    tm = min(tm, m)
    k_total = w.shape[-2]
    assert sum(a.shape[1] for a in a_list) == k_total
    assert col_off % tn == 0 and n_cols % tn == 0 and m % tm == 0
    off_blocks = col_off // tn
    in_specs = [pl.BlockSpec((tm, a.shape[1]), lambda i, j: (i, 0)) for a in a_list]
    in_specs.append(_w_spec(w, layer, (k_total, tn), lambda i, j: (0, j + off_blocks)))
    in_specs += [pl.BlockSpec(bs, im) for (_, bs, im) in extras]
    out_specs = [pl.BlockSpec(bs, im) for (_, _, bs, im) in outs]
    out_shape = [jax.ShapeDtypeStruct(s, dt) for (s, dt, _, _) in outs]
    res = pl.pallas_call(
        functools.partial(_mm_kernel, n_a=len(a_list), n_extra=len(extras), epilogue=epilogue),
        grid=(m // tm, n_cols // tn),
        in_specs=in_specs,
        out_specs=out_specs,
        out_shape=out_shape,
        compiler_params=_cparams("parallel", "arbitrary"),
        name=name,
    )(*a_list, w, *[e[0] for e in extras])
    return res


def _epi_scale(acc, extra_refs, out_refs, *, scale):
    out_refs[0][...] = (acc * scale if scale != 1.0 else acc).astype(out_refs[0].dtype)


def _epi_silu(acc, extra_refs, out_refs):
    out_refs[0][...] = _silu(acc).astype(out_refs[0].dtype)


def _epi_residual(acc, extra_refs, out_refs):
    out_refs[0][...] = extra_refs[0][...] + acc


def _epi_rotary(acc, extra_refs, out_refs):
    cos = extra_refs[0][...]
    sin = extra_refs[1][...]
    half = RET_DIM // 2
    tn = acc.shape[1]
    is_q = pl.program_id(1) * tn < RET_WIDTH
    scale = jnp.where(is_q, 1.0, RET_DIM ** -0.5).astype(F32)
    for hd in range(tn // RET_DIM):
        lo = hd * RET_DIM
        x1 = acc[:, lo:lo + half]
        x2 = acc[:, lo + half:lo + RET_DIM]
        out_refs[0][:, lo:lo + half] = ((x1 * cos - x2 * sin) * scale).astype(BF16)
        out_refs[0][:, lo + half:lo + RET_DIM] = ((x1 * sin + x2 * cos) * scale).astype(BF16)


def _epi_segments(acc, extra_refs, out_refs, *, segments):
    j = pl.program_id(1)
    for lo, hi, epi in segments:
        pl.when((j >= lo) & (j < hi))(functools.partial(epi, acc, extra_refs, out_refs))


def _epi_log_sigmoid(acc, extra_refs, out_refs):
    z = acc + extra_refs[0][...]
    out_refs[0][...] = jnp.minimum(z, 0.0) - jnp.log1p(jnp.exp(-jnp.abs(z)))


def _epi_softplus(acc, extra_refs, out_refs):
    out_refs[0][...] = _softplus(acc + extra_refs[0][...])


def _simple_out(m, n, tm, tn, dtype):
    tm = min(tm, m)
    return [((m, n), dtype, (tm, tn), lambda i, j: (i, j))]


def _gateup_kernel(h_ref, wg_ref, wu_ref, o_ref):
    h = h_ref[...]
    g = _dot(h, wg_ref[...])
    u = _dot(h, wu_ref[...])
    o_ref[...] = (_silu(g) * u).astype(o_ref.dtype)


def gateup(h, wg, wu, layer, tn=512):
    m, k = h.shape
    n = wg.shape[-1]
    tm = min(TM, m)
    return pl.pallas_call(
        _gateup_kernel,
        grid=(m // tm, n // tn),
        in_specs=[pl.BlockSpec((tm, k), lambda i, j: (i, 0)),
                  _w_spec(wg, layer, (k, tn), lambda i, j: (0, j)),
                  _w_spec(wu, layer, (k, tn), lambda i, j: (0, j))],
        out_specs=pl.BlockSpec((tm, tn), lambda i, j: (i, j)),
        out_shape=jax.ShapeDtypeStruct((m, n), BF16),
        compiler_params=_cparams("parallel", "arbitrary"),
        name="ffn_gateup",
    )(h, wg, wu)


def _mm_k_residual_kernel(a_ref, w_ref, x_ref, o_ref, acc_ref):
    kk = pl.program_id(2)
    part = _dot(a_ref[...], w_ref[...])

    @pl.when(kk == 0)
    def _():
        acc_ref[...] = part

    @pl.when(kk > 0)
    def _():
        acc_ref[...] += part

    @pl.when(kk == pl.num_programs(2) - 1)
    def _():
        o_ref[...] = x_ref[...] + acc_ref[...]


def matmul_k_residual(a, w, x, *, tk, tn=1024, name, layer=0):
    m, k = a.shape
    n = w.shape[-1]
    tm = min(TM, m)
    assert w.shape[-2] == k and k % tk == 0 and n % tn == 0
    return pl.pallas_call(
        _mm_k_residual_kernel,
        grid=(m // tm, n // tn, k // tk),
        in_specs=[pl.BlockSpec((tm, tk), lambda i, j, kk: (i, kk)),
                  _w_spec(w, layer, (tk, tn), lambda i, j, kk: (kk, j)),
                  pl.BlockSpec((tm, tn), lambda i, j, kk: (i, j))],
        out_specs=pl.BlockSpec((tm, tn), lambda i, j, kk: (i, j)),
        out_shape=jax.ShapeDtypeStruct((m, n), F32),
        scratch_shapes=[pltpu.VMEM((tm, tn), F32)],
        compiler_params=_cparams("parallel", "parallel", "arbitrary"),
        name=name,
    )(a, w, x)


def _cumsum_kernel(x_ref, o_ref, carry_ref):
    @pl.when(pl.program_id(0) == 0)
    def _():
        carry_ref[...] = jnp.zeros_like(carry_ref)

    n = x_ref.shape[0]
    tri = _lower_tri(n).astype(F32)
    c = _dot_f32(tri, x_ref[...]) + carry_ref[0:1, :]
    o_ref[...] = c
    carry_ref[...] = jnp.broadcast_to(c[n - 1:n, :], carry_ref.shape)


def cumsum_rows(x):
    m, n = x.shape
    tb = min(CUMSUM_BLOCK, m)
    return pl.pallas_call(
        _cumsum_kernel,
        grid=(m // tb,),
        in_specs=[pl.BlockSpec((tb, n), lambda i: (i, 0))],
        out_specs=pl.BlockSpec((tb, n), lambda i: (i, 0)),
        out_shape=jax.ShapeDtypeStruct((m, n), F32),
        scratch_shapes=[pltpu.VMEM((SUBLANES, n), F32)],
        compiler_params=_cparams("arbitrary"),
        name="cumsum_rows",
    )(x)


def _retention_kernel(lg_ref, q_ref, k_ref, v_ref, g_ref, o_ref, state_ref, *, chunk, n_sub):
    head = pl.program_id(0)

    @pl.when(pl.program_id(1) == 0)
    def _():
        state_ref[...] = jnp.zeros_like(state_ref)

    lg = lg_ref[head]
    L = chunk
    ti = lax.broadcasted_iota(jnp.int32, (L, L), 0)
    si = lax.broadcasted_iota(jnp.int32, (L, L), 1)
    causal = ti >= si
    diff = jnp.where(causal, (ti - si).astype(F32), 0.0)
    decay_in = jnp.where(causal, jnp.exp(lg * diff), 0.0)
    idx = lax.broadcasted_iota(jnp.int32, (L, 1), 0).astype(F32)
    q_decay = jnp.exp(lg * (idx + 1.0))
    k_decay = jnp.exp(lg * (L - 1.0 - idx))
    chunk_decay = jnp.exp(lg * float(L))

    for u in range(n_sub):
        rows = slice(u * L, (u + 1) * L)
        q = q_ref[rows, :]
        k = k_ref[rows, :]
        v = v_ref[rows, :]
        state = state_ref[...]
        scores = _dot_nt(q, k) * decay_in
        inner = _dot(scores.astype(BF16), v)
        cross = _dot(q, state.astype(BF16)) * q_decay
        out = inner + cross
        kd = (k.astype(F32) * k_decay).astype(BF16)
        state_ref[...] = state * chunk_decay + _dot_tn(kd, v)
        ms = jnp.mean(out * out, axis=-1, keepdims=True)
        o_ref[rows, :] = (out * lax.rsqrt(ms + EPS) * g_ref[rows, :].astype(F32)).astype(o_ref.dtype)


def retention(proj, log_gamma):
    s = proj.shape[0]
    chunk = min(RET_CHUNK, s)
    tb = min(RET_BLOCK, s)
    return pl.pallas_call(
        functools.partial(_retention_kernel, chunk=chunk, n_sub=tb // chunk),
        grid=(RET_HEADS, s // tb),
        in_specs=[pl.BlockSpec(memory_space=pltpu.SMEM),
                  pl.BlockSpec((tb, RET_DIM), lambda h, c: (c, h)),
                  pl.BlockSpec((tb, RET_DIM), lambda h, c: (c, RET_HEADS + h)),
                  pl.BlockSpec((tb, RET_DIM), lambda h, c: (c, 2 * RET_HEADS + h)),
                  pl.BlockSpec((tb, RET_DIM), lambda h, c: (c, 3 * RET_HEADS + h))],
        out_specs=pl.BlockSpec((tb, RET_DIM), lambda h, c: (c, h)),
        out_shape=jax.ShapeDtypeStruct((s, RET_WIDTH), BF16),
        scratch_shapes=[pltpu.VMEM((RET_DIM, RET_DIM), F32)],
        compiler_params=_cparams("parallel", "arbitrary"),
        name="retention",
    )(log_gamma, proj, proj, proj, proj)


def _split3(x):
    hi = x.astype(BF16)
    r1 = x - hi.astype(F32)
    mid = r1.astype(BF16)
    lo = (r1 - mid.astype(F32)).astype(BF16)
    return hi, mid, lo


def _fox_prep_kernel(q_ref, k_ref, v_ref, c_ref, qa_ref, ka_ref, va_ref):
    tb = q_ref.shape[0]
    d = FOX_HEAD_DIM
    lane = lax.broadcasted_iota(jnp.int32, (tb, d), 1)
    one = jnp.ones((tb, d), F32)
    zero = jnp.zeros((tb, d), F32)
    c_all = c_ref[...] * LOG2E
    for hd in range(FOX_HEADS):
        hi, mid, lo = [jnp.broadcast_to(part.astype(F32), (tb, d)) for part in _split3(c_all[:, hd:hd + 1])]
        aug_q = jnp.where(lane == 0, hi, jnp.where(lane == 1, mid, jnp.where(lane == 2, lo,
                          jnp.where(lane < 6, one, zero))))
        aug_k = jnp.where(lane < 3, one, jnp.where(lane == 3, -hi, jnp.where(lane == 4, -mid,
                          jnp.where(lane == 5, -lo, zero))))
        src = slice(hd * d, (hd + 1) * d)
        data = slice(2 * hd * d, (2 * hd + 1) * d)
        aug = slice((2 * hd + 1) * d, (2 * hd + 2) * d)
        qa_ref[:, data] = q_ref[:, src]
        qa_ref[:, aug] = aug_q.astype(BF16)
        ka_ref[:, data] = k_ref[:, src]
        ka_ref[:, aug] = aug_k.astype(BF16)
        va_ref[:, data] = v_ref[:, src]
        va_ref[:, aug] = one.astype(BF16)


def fox_prep(proj, c, tb=512):
    s = proj.shape[0]
    tb = min(tb, s)
    w = FOX_WIDTH
    first = 4 * RET_WIDTH // w
    out = jax.ShapeDtypeStruct((s, 2 * w), BF16)
    return pl.pallas_call(
        _fox_prep_kernel,
        grid=(s // tb,),
        in_specs=[pl.BlockSpec((tb, w), lambda i: (i, first)),
                  pl.BlockSpec((tb, w), lambda i: (i, first + 1)),
                  pl.BlockSpec((tb, w), lambda i: (i, first + 2)),
                  pl.BlockSpec((tb, LANES), lambda i: (i, 0))],
        out_specs=[pl.BlockSpec((tb, 2 * w), lambda i: (i, 0))] * 3,
        out_shape=[out, out, out],
        compiler_params=_cparams("parallel"),
        name="fox_prep",
    )(proj, proj, proj, c)


def _fox_kernel(q_ref, k_ref, v_ref, o_ref, s0_ref, s1_ref, m_ref, acc_ref, *, blk, sub):
    qi = pl.program_id(1)
    d = FOX_HEAD_DIM
    q = q_ref[...]

    def scores(jsub):
        start = pl.multiple_of(jsub * sub, sub)
        return _dot_nt(q, k_ref[pl.ds(start, sub), :])

    def process(s_ref, jsub, mask_off):
        start = pl.multiple_of(jsub * sub, sub)
        s = s_ref[...]
        if mask_off is not None:
            r = lax.broadcasted_iota(jnp.int32, (blk, sub), 0)
            c = lax.broadcasted_iota(jnp.int32, (blk, sub), 1)
            s = jnp.where(r >= c + mask_off, s, NEG_BIG)
        m_old = m_ref[...]
        m_new = jnp.maximum(m_old, jnp.max(s, axis=1, keepdims=True))
        alpha = jnp.exp2(m_old - m_new)
        p = jnp.concatenate([jnp.exp2(s[:, u * d:(u + 1) * d] - m_new) for u in range(sub // d)], axis=1)
        pv = _dot(p.astype(BF16), v_ref[pl.ds(start, sub), :])
        acc_ref[:, :d] = alpha * acc_ref[:, :d] + pv[:, :d]
        acc_ref[:, d:] = alpha * acc_ref[:, d:] + pv[:, d:]
        m_ref[...] = m_new

    m_ref[...] = jnp.full_like(m_ref, NEG_BIG)
    acc_ref[...] = jnp.zeros_like(acc_ref)
    s0_ref[...] = scores(0)

    def body(t, carry):
        s1_ref[...] = scores(2 * t + 1)
        process(s0_ref, 2 * t, None)
        s0_ref[...] = scores(2 * t + 2)
        process(s1_ref, 2 * t + 1, None)
        return carry

    lax.fori_loop(0, qi, body, 0)
    s1_ref[...] = scores(2 * qi + 1)
    process(s0_ref, 2 * qi, 0)
    process(s1_ref, 2 * qi + 1, sub)
    o_ref[...] = (acc_ref[:, :d] / acc_ref[:, d:]).astype(o_ref.dtype)


def fox_attention(qa, ka, va):
    s = qa.shape[0]
    blk = min(FOX_BLOCK, s)
    sub = blk // 2
    d = FOX_HEAD_DIM
    return pl.pallas_call(
        functools.partial(_fox_kernel, blk=blk, sub=sub),
        grid=(FOX_HEADS, s // blk),
        in_specs=[pl.BlockSpec((blk, 2 * d), lambda h, i: (i, h)),
                  pl.BlockSpec((s, 2 * d), lambda h, i: (0, h)),
                  pl.BlockSpec((s, 2 * d), lambda h, i: (0, h))],
        out_specs=pl.BlockSpec((blk, d), lambda h, i: (i, h)),
        out_shape=jax.ShapeDtypeStruct((s, FOX_WIDTH), BF16),
        scratch_shapes=[pltpu.VMEM((blk, sub), F32), pltpu.VMEM((blk, sub), F32),
                        pltpu.VMEM((blk, d), F32), pltpu.VMEM((blk, 2 * d), F32)],
        compiler_params=_cparams("parallel", "arbitrary"),
        name="fox_attention",
    )(qa, ka, va)


def _rms_scale(x, g):
    return x * lax.rsqrt(jnp.mean(x * x, axis=-1, keepdims=True) + EPS) * g


def _xattn_kernel(x_ref, g_in_ref, g_out_ref, wq_ref, kt_ref, v_ref, wo_ref, o_ref, h_out_ref):
    x = x_ref[...]
    h = _rms_scale(x, g_in_ref[...]).astype(BF16)
    q = _dot(h, wq_ref[...]).astype(BF16)
    heads = []
    for hd in range(XA_HEADS):
        cols = slice(hd * XA_HEAD_DIM, (hd + 1) * XA_HEAD_DIM)
        logits = _dot(q[:, cols], kt_ref[hd]) * (XA_HEAD_DIM ** -0.5)
        mx = jnp.max(logits, axis=-1, keepdims=True)
        e = jnp.exp(logits - mx)
        p = e / jnp.sum(e, axis=-1, keepdims=True)
        heads.append(_dot(p.astype(BF16), v_ref[:, cols]))
    o = jnp.concatenate(heads, axis=-1).astype(BF16)
    x_new = x + _dot(o, wo_ref[...])
    o_ref[...] = x_new
    h_out_ref[...] = _rms_scale(x_new, g_out_ref[...]).astype(BF16)


def cross_attention(x, g_in, g_out, wq, k_t, v, wo, layer, tm=256):
    s, d = x.shape
    tm = min(tm, s)
    mem_len = v.shape[0]
    const2 = lambda i: (0, 0)
    row = lambda i: (i, 0)
    return pl.pallas_call(
        _xattn_kernel,
        grid=(s // tm,),
        in_specs=[pl.BlockSpec((tm, d), row),
                  pl.BlockSpec((1, d), const2),
                  pl.BlockSpec((1, d), const2),
                  _w_spec(wq, layer, (d, XA_WIDTH), const2),
                  pl.BlockSpec((XA_HEADS, XA_HEAD_DIM, mem_len), lambda i: (0, 0, 0)),
                  pl.BlockSpec((mem_len, XA_WIDTH), const2),
                  _w_spec(wo, layer, (XA_WIDTH, d), const2)],
        out_specs=[pl.BlockSpec((tm, d), row), pl.BlockSpec((tm, d), row)],
        out_shape=[jax.ShapeDtypeStruct((s, d), F32), jax.ShapeDtypeStruct((s, d), BF16)],
        compiler_params=_cparams("parallel"),
        name="cross_attention",
    )(x, g_in.astype(F32).reshape(1, d), g_out.astype(F32).reshape(1, d), wq, k_t, v, wo)


def _conv_silu(buf_ref, blk, w, b, L):
    buf_ref[SUBLANES:SUBLANES + L, :] = blk
    y = b
    for kk in range(SSM_CONV):
        lo = SUBLANES - (SSM_CONV - 1) + kk
        y = y + w[kk:kk + 1, :] * buf_ref[lo:lo + L, :]
    buf_ref[0:SUBLANES, :] = blk[L - SUBLANES:, :]
    return _silu(y)


def _ssd_kernel(x_ref, b_ref, c_ref, wx_ref, wb_ref, wc_ref, bx_ref, bb_ref, bc_ref,
                dt_ref, dtt_ref, a_ref, at_ref, dskip_ref, z_ref, gnw_ref,
                y_ref, state_ref, xbuf_ref, bbuf_ref, cbuf_ref, *, L):
    @pl.when(pl.program_id(1) == 0)
    def _():
        state_ref[...] = jnp.zeros_like(state_ref)
        xbuf_ref[...] = jnp.zeros_like(xbuf_ref)
        bbuf_ref[...] = jnp.zeros_like(bbuf_ref)
        cbuf_ref[...] = jnp.zeros_like(cbuf_ref)

    xs = _conv_silu(xbuf_ref, x_ref[...].astype(F32), wx_ref[...], bx_ref[...], L)
    bm = _conv_silu(bbuf_ref, b_ref[...].astype(F32), wb_ref[...], bb_ref[...], L)
    cm = _conv_silu(cbuf_ref, c_ref[...].astype(F32), wc_ref[...], bc_ref[...], L)
    bm16 = bm.astype(BF16)
    cm16 = cm.astype(BF16)

    causal = _lower_tri(L)
    tri = causal.astype(F32)
    dt = dt_ref[...]
    da = dt * a_ref[...]
    acs = _dot_f32(tri, da)
    acs_t = lax.dot_general(dtt_ref[...] * at_ref[...], tri, (((1,), (1,)), ((), ())),
                            preferred_element_type=F32, precision=lax.Precision.HIGHEST)
    acs_last = acs[L - 1:L, :]
    exp_acs = jnp.exp(acs)
    tail = jnp.exp(acs_last - acs)
    exp_last = jnp.exp(acs_last)

    cb = _dot_nt(cm16, bm16)
    state = state_ref[...]
    y_off = _dot(cm16, state.astype(BF16))

    lane = lax.broadcasted_iota(jnp.int32, (L, LANES), 1)
    first_half = lane < SSM_HEAD_DIM
    lane_row = lax.broadcasted_iota(jnp.int32, (1, LANES), 1) < SSM_HEAD_DIM
    y_parts = []
    xt_parts = []
    sdec_parts = []
    for pr in range(SSM_HPG // 2):
        e0, e1 = 2 * pr, 2 * pr + 1
        cols = slice(pr * LANES, (pr + 1) * LANES)

        def pair(col_arr):
            return jnp.where(first_half,
                             jnp.broadcast_to(col_arr[:, e0:e0 + 1], (L, LANES)),
                             jnp.broadcast_to(col_arr[:, e1:e1 + 1], (L, LANES)))

        x_pair = xs[:, cols]
        xdt = x_pair * pair(dt)
        m0 = jnp.exp(jnp.where(causal, acs[:, e0:e0 + 1] - acs_t[e0:e0 + 1, :], NEG_BIG)) * cb
        m1 = jnp.exp(jnp.where(causal, acs[:, e1:e1 + 1] - acs_t[e1:e1 + 1, :], NEG_BIG)) * cb
        mcat = jnp.concatenate([m0, m1], axis=1).astype(BF16)
        xdt16 = xdt.astype(BF16)
        zero = jnp.zeros_like(xdt16)
        bd = jnp.concatenate([jnp.where(first_half, xdt16, zero),
                              jnp.where(first_half, zero, xdt16)], axis=0)
        y_diag = _dot(mcat, bd)
        y_parts.append(y_diag + y_off[:, cols] * pair(exp_acs))
        xt_parts.append((xdt * pair(tail)).astype(BF16))
        sdec_parts.append(jnp.where(lane_row,
                                    jnp.broadcast_to(exp_last[:, e0:e0 + 1], (1, LANES)),
                                    jnp.broadcast_to(exp_last[:, e1:e1 + 1], (1, LANES))))
    y = jnp.concatenate(y_parts, axis=1)
    xt = jnp.concatenate(xt_parts, axis=1)
    sdec = jnp.concatenate(sdec_parts, axis=1)
    state_ref[...] = state * sdec + _dot_tn(bm16, xt)

    y = (y + xs * dskip_ref[...]) * z_ref[...].astype(F32)
    ms = jnp.mean(y * y, axis=-1, keepdims=True)
    y_ref[...] = (y * lax.rsqrt(ms + EPS) * gnw_ref[...]).astype(y_ref.dtype)


def ssd_mixer(proj, conv_w, conv_b, dt_g, dt_gt, a_g, a_gt, dskip_row, gn_w):
    s = proj.shape[0]
    L = min(SSD_CHUNK, s)
    gw = SSM_GROUP_WIDTH
    n = SSM_STATE
    xb = SSM_INNER // n
    cbk = xb + SSM_GROUPS
    pz = SSM_INNER // gw
    pb = SSM_INNER // n
    return pl.pallas_call(
        functools.partial(_ssd_kernel, L=L),
        grid=(SSM_GROUPS, s // L),
        in_specs=[pl.BlockSpec((L, gw), lambda g, c: (c, pz + g)),
                  pl.BlockSpec((L, n), lambda g, c: (c, pb + xb + g)),
                  pl.BlockSpec((L, n), lambda g, c: (c, pb + cbk + g)),
                  pl.BlockSpec((SSM_CONV, gw), lambda g, c: (0, g)),
                  pl.BlockSpec((SSM_CONV, n), lambda g, c: (0, xb + g)),
                  pl.BlockSpec((SSM_CONV, n), lambda g, c: (0, cbk + g)),
                  pl.BlockSpec((1, gw), lambda g, c: (0, g)),
                  pl.BlockSpec((1, n), lambda g, c: (0, xb + g)),
                  pl.BlockSpec((1, n), lambda g, c: (0, cbk + g)),
                  pl.BlockSpec((None, L, SSM_HPG), lambda g, c: (g, c, 0)),
                  pl.BlockSpec((None, SSM_HPG, L), lambda g, c: (g, 0, c)),
                  pl.BlockSpec((None, 1, SSM_HPG), lambda g, c: (g, 0, 0)),
                  pl.BlockSpec((None, SSM_HPG, 1), lambda g, c: (g, 0, 0)),
                  pl.BlockSpec((1, gw), lambda g, c: (0, g)),
                  pl.BlockSpec((L, gw), lambda g, c: (c, g)),
                  pl.BlockSpec((1, gw), lambda g, c: (0, g))],
        out_specs=pl.BlockSpec((L, gw), lambda g, c: (c, g)),
        out_shape=jax.ShapeDtypeStruct((s, SSM_INNER), BF16),
        scratch_shapes=[pltpu.VMEM((n, gw), F32),
                        pltpu.VMEM((L + SUBLANES, gw), F32),
                        pltpu.VMEM((L + SUBLANES, n), F32),
                        pltpu.VMEM((L + SUBLANES, n), F32)],
        compiler_params=_cparams("parallel", "arbitrary"),
        name="ssd_mixer",
    )(proj, proj, proj, conv_w, conv_w, conv_w, conv_b, conv_b, conv_b,
      dt_g, dt_gt, a_g, a_gt, dskip_row, proj, gn_w)


PROJ_TN = 512
EV_PROJ_COLS = 4 * RET_WIDTH + 3 * FOX_WIDTH
OD_PROJ_COLS = SSM_INNER + SSM_CONV_DIM


def _retention_fox_layer(x, norm_g, w_in16, i, b_f, w_out16):
    s = x.shape[0]
    tm = min(TM, s)
    h = rmsnorm(x, norm_g, BF16)

    half = RET_DIM // 2
    inv = ROPE_BASE ** (-jnp.arange(half, dtype=F32) / half)
    ang = jnp.arange(s).astype(F32)[:, None] * inv[None, :]
    cos, sin = jnp.cos(ang), jnp.sin(ang)

    plain = functools.partial(_epi_scale, scale=1.0)
    fq_scale = functools.partial(_epi_scale, scale=LOG2E * FOX_HEAD_DIM ** -0.5)
    bounds = [0, 2 * RET_WIDTH, 3 * RET_WIDTH, 4 * RET_WIDTH, 4 * RET_WIDTH + FOX_WIDTH, EV_PROJ_COLS]
    epis = [_epi_rotary, plain, _epi_silu, fq_scale, plain]
    segments = [(lo // PROJ_TN, hi // PROJ_TN, e) for lo, hi, e in zip(bounds[:-1], bounds[1:], epis)]
    (proj,) = matmul([h], w_in16, layer=i, col_off=0, n_cols=EV_PROJ_COLS, tn=PROJ_TN,
                     epilogue=functools.partial(_epi_segments, segments=segments),
                     extras=[(cos, (tm, half), lambda i_, j: (i_, 0)), (sin, (tm, half), lambda i_, j: (i_, 0))],
                     outs=_simple_out(s, EV_PROJ_COLS, tm, PROJ_TN, BF16), name="ev_proj")

    w_ff = jnp.pad(w_in16[i, :, EV_PROJ_COLS:EV_PROJ_COLS + FOX_HEADS], ((0, 0), (0, LANES - FOX_HEADS)))
    b_pad = jnp.pad(b_f.astype(F32), (0, LANES - FOX_HEADS)).reshape(1, LANES)
    (log_f,) = matmul([h], w_ff, col_off=0, n_cols=LANES, tn=LANES, epilogue=_epi_log_sigmoid,
                      extras=[(b_pad, (1, LANES), lambda i_, j: (0, 0))],
                      outs=_simple_out(s, LANES, tm, LANES, F32), name="ev_proj_forget")
    c = cumsum_rows(log_f)

    log_gamma = jnp.log1p(-jnp.exp2(-5.0 - jnp.arange(RET_HEADS, dtype=F32)))
    ret = retention(proj, log_gamma)
    fox = fox_attention(*fox_prep(proj, c))

    (x_new,) = matmul([ret, fox], w_out16, layer=i, col_off=0, n_cols=D_MODEL, tn=1024, epilogue=_epi_residual,
                      extras=[(x, (tm, 1024), lambda i_, j: (i_, j))],
                      outs=_simple_out(s, D_MODEL, tm, 1024, F32), name="ev_out_proj")
    return x_new


def _mamba2_layer(x, norm_g, w_in16, i, conv_w, conv_b, dt_bias, a_log, d_skip, gn_w, w_out16):
    s = x.shape[0]
    tm = min(TM, s)
    h = rmsnorm(x, norm_g, BF16)
    segments = [(0, SSM_INNER // PROJ_TN, _epi_silu),
                (SSM_INNER // PROJ_TN, OD_PROJ_COLS // PROJ_TN, functools.partial(_epi_scale, scale=1.0))]
    (proj,) = matmul([h], w_in16, layer=i, col_off=0, n_cols=OD_PROJ_COLS, tn=PROJ_TN,
                     epilogue=functools.partial(_epi_segments, segments=segments),
                     outs=_simple_out(s, OD_PROJ_COLS, tm, PROJ_TN, BF16), name="od_proj")
    (dt,) = matmul([h], w_in16, layer=i, col_off=OD_PROJ_COLS, n_cols=SSM_HEADS, tn=SSM_HEADS,
                   epilogue=_epi_softplus,
                   extras=[(dt_bias.astype(F32).reshape(1, SSM_HEADS), (1, SSM_HEADS), lambda i_, j: (0, 0))],
                   outs=_simple_out(s, SSM_HEADS, tm, SSM_HEADS, F32), name="od_proj_dt")

    dt_g = dt.reshape(s, SSM_GROUPS, SSM_HPG).transpose(1, 0, 2)
    dt_gt = dt_g.transpose(0, 2, 1)
    a = -jnp.exp(a_log.astype(F32))
    a_g = a.reshape(SSM_GROUPS, 1, SSM_HPG)
    a_gt = a.reshape(SSM_GROUPS, SSM_HPG, 1)
    dskip_row = jnp.repeat(d_skip.astype(F32), SSM_HEAD_DIM).reshape(1, SSM_INNER)
    y = ssd_mixer(proj, conv_w.astype(F32), conv_b.astype(F32).reshape(1, SSM_CONV_DIM),
                  dt_g, dt_gt, a_g, a_gt, dskip_row, gn_w.astype(F32).reshape(1, SSM_INNER))
    return matmul_k_residual(y, w_out16, x, layer=i, tk=2048, name="od_out_proj")


def _cross_attention_layer(x, g_in, g_out, mem_n, wq16, wk16, wv16, wo16, layer):
    m = mem_n.shape[0]
    plain = functools.partial(_epi_scale, scale=1.0)
    (k,) = matmul([mem_n], wk16, layer=layer, col_off=0, n_cols=XA_WIDTH, tn=XA_WIDTH, epilogue=plain,
                  outs=_simple_out(m, XA_WIDTH, m, XA_WIDTH, BF16), name="xa_proj_k")
    (v,) = matmul([mem_n], wv16, layer=layer, col_off=0, n_cols=XA_WIDTH, tn=XA_WIDTH, epilogue=plain,
                  outs=_simple_out(m, XA_WIDTH, m, XA_WIDTH, BF16), name="xa_proj_v")
    k_t = k.reshape(m, XA_HEADS, XA_HEAD_DIM).transpose(1, 2, 0)
    return cross_attention(x, g_in, g_out, wq16, k_t, v, wo16, layer)


def kernel(x, mem, mem_norm, ev_mix_norm, ev_w_in, ev_b_f, ev_w_out, od_mix_norm, od_w_in, od_conv_w, od_conv_b,
           od_dt_bias, od_a_log, od_d_skip, od_gn_w, od_w_out, xa_norm, xa_wq, xa_wk, xa_wv, xa_wo, ffn_norm,
           ffn_w_gate, ffn_w_up, ffn_w_down, final_norm):
    b, s, d = x.shape
    assert b == 1 and d == D_MODEL
    xs = x.reshape(s, d).astype(F32)
    mem_n = rmsnorm(mem.reshape(mem.shape[1], d).astype(F32), mem_norm, BF16)

    ff_pad = D_FF_PAD - D_FF
    ev_w_in16, ev_w_out16 = ev_w_in.astype(BF16), ev_w_out.astype(BF16)
    od_w_in16, od_w_out16 = od_w_in.astype(BF16), od_w_out.astype(BF16)
    xa_wq16, xa_wk16, xa_wv16, xa_wo16 = [w.astype(BF16) for w in (xa_wq, xa_wk, xa_wv, xa_wo)]
    wg16 = jnp.pad(ffn_w_gate.astype(BF16), ((0, 0), (0, 0), (0, ff_pad)))
    wu16 = jnp.pad(ffn_w_up.astype(BF16), ((0, 0), (0, 0), (0, ff_pad)))
    wd16 = jnp.pad(ffn_w_down.astype(BF16), ((0, 0), (0, ff_pad), (0, 0)))

    depth = xa_norm.shape[0]
    for layer in range(depth):
        i = layer // 2
        if layer % 2 == 0:
            xs = _retention_fox_layer(xs, ev_mix_norm[i], ev_w_in16, i, ev_b_f[i], ev_w_out16)
        else:
            xs = _mamba2_layer(xs, od_mix_norm[i], od_w_in16, i, od_conv_w[i], od_conv_b[i], od_dt_bias[i],
                               od_a_log[i], od_d_skip[i], od_gn_w[i], od_w_out16)
        xs, h_ffn = _cross_attention_layer(xs, xa_norm[layer], ffn_norm[layer], mem_n,
                                           xa_wq16, xa_wk16, xa_wv16, xa_wo16, layer)
        a = gateup(h_ffn, wg16, wu16, layer)
        xs = matmul_k_residual(a, wd16, xs, layer=layer, tk=DOWN_TK, name="ffn_down")
    out = rmsnorm(xs, final_norm, F32)
    return out.reshape(b, s, d)
```

```python
import functools
import math

import jax
import jax.numpy as jnp
from jax import lax
from jax.experimental import pallas as pl
from jax.experimental.pallas import tpu as pltpu

F32 = jnp.float32
BF16 = jnp.bfloat16

D_MODEL = 4096
EPS = 1e-6
RET_HEADS = 8
RET_DIM = 256
RET_WIDTH = RET_HEADS * RET_DIM
ROPE_BASE = 10000.0
FOX_HEAD_DIM = 128
FOX_WIDTH = 2048
FOX_HEADS = FOX_WIDTH // FOX_HEAD_DIM
SSM_INNER = 8192
SSM_HEAD_DIM = 64
SSM_HEADS = SSM_INNER // SSM_HEAD_DIM
SSM_STATE = 128
SSM_GROUPS = 8
SSM_HPG = SSM_HEADS // SSM_GROUPS
SSM_GROUP_WIDTH = SSM_INNER // SSM_GROUPS
SSM_CONV = 4
SSM_CONV_DIM = SSM_INNER + 2 * SSM_GROUPS * SSM_STATE
XA_HEADS = 4
XA_HEAD_DIM = 128
XA_WIDTH = XA_HEADS * XA_HEAD_DIM
D_FF = 11008

V7X_VMEM_BYTES = 64 * 1024 * 1024
VMEM_LIMIT_BYTES = V7X_VMEM_BYTES - 8 * 1024 * 1024
LANES = 128
SUBLANES = 8

TM = 1024
D_FF_PAD = 11264
DOWN_TK = 2816
RET_CHUNK = 256
RET_BLOCK = 512
FOX_BLOCK = 1024
SSD_CHUNK = 128
CUMSUM_BLOCK = 256
NEG_BIG = -1e30
LOG2E = 1.4426950408889634


def _cparams(*sem):
    return pltpu.CompilerParams(dimension_semantics=sem, vmem_limit_bytes=VMEM_LIMIT_BYTES)


def _silu(x):
    h = 0.5 * x
    return h + h * jnp.tanh(h)


def _softplus(x):
    return jnp.maximum(x, 0.0) + jnp.log1p(jnp.exp(-jnp.abs(x)))


def _dot(a, b):
    return jnp.dot(a, b, preferred_element_type=F32)


def _dot_nt(a, b):
    return lax.dot_general(a, b, (((1,), (1,)), ((), ())), preferred_element_type=F32)


def _dot_tn(a, b):
    return lax.dot_general(a, b, (((0,), (0,)), ((), ())), preferred_element_type=F32)


def _dot_f32(a, b):
    return jnp.dot(a, b, preferred_element_type=F32, precision=lax.Precision.HIGHEST)


def _lower_tri(n):
    r = lax.broadcasted_iota(jnp.int32, (n, n), 0)
    c = lax.broadcasted_iota(jnp.int32, (n, n), 1)
    return r >= c


def _rmsnorm_kernel(x_ref, g_ref, o_ref):
    x = x_ref[...]
    ms = jnp.mean(x * x, axis=-1, keepdims=True)
    o_ref[...] = (x * lax.rsqrt(ms + EPS) * g_ref[...]).astype(o_ref.dtype)


def rmsnorm(x, g, out_dtype, tm=512):
    m, d = x.shape
    tm = min(tm, m)
    return pl.pallas_call(
        _rmsnorm_kernel,
        grid=(m // tm,),
        in_specs=[pl.BlockSpec((tm, d), lambda i: (i, 0)),
                  pl.BlockSpec((1, d), lambda i: (0, 0))],
        out_specs=pl.BlockSpec((tm, d), lambda i: (i, 0)),
        out_shape=jax.ShapeDtypeStruct((m, d), out_dtype),
        compiler_params=_cparams("parallel"),
        name="rmsnorm",
    )(x, g.reshape(1, d).astype(F32))


def _cast_pad_kernel(x_ref, o_ref, *, n_row_blocks_in, cols_in):
    r = pl.program_id(1)
    rows, cols_out = o_ref.shape
    vals = jnp.where(r < n_row_blocks_in, x_ref[...], 0.0).astype(o_ref.dtype)
    o_ref[:, :cols_in] = vals
    if cols_out > cols_in:
        o_ref[:, cols_in:] = jnp.zeros((rows, cols_out - cols_in), o_ref.dtype)


def cast_pad_bf16(w, rows_out=None, cols_out=None, tr=256):
    layers, rows_in, cols_in = w.shape
    rows_out = rows_out or rows_in
    cols_out = cols_out or cols_in
    assert rows_in % tr == 0 and rows_out % tr == 0
    n_in = rows_in // tr
    return pl.pallas_call(
        functools.partial(_cast_pad_kernel, n_row_blocks_in=n_in, cols_in=cols_in),
        grid=(layers, rows_out // tr),
        in_specs=[pl.BlockSpec((None, tr, cols_in), lambda l, r: (l, jnp.minimum(r, n_in - 1), 0))],
        out_specs=pl.BlockSpec((None, tr, cols_out), lambda l, r: (l, r, 0)),
        out_shape=jax.ShapeDtypeStruct((layers, rows_out, cols_out), BF16),
        compiler_params=_cparams("parallel", "parallel"),
        name="cast_pad_bf16",
    )(w)


def _mm_kernel(*refs, n_a, n_extra, epilogue):
    a_refs = refs[:n_a]
    w_ref = refs[n_a]
    extra_refs = refs[n_a + 1:n_a + 1 + n_extra]
    out_refs = refs[n_a + 1 + n_extra:]
    acc = None
    off = 0
    for a_ref in a_refs:
        k = a_ref.shape[1]
        part = _dot(a_ref[...], w_ref[off:off + k, :])
        acc = part if acc is None else acc + part
        off += k
    epilogue(acc, extra_refs, out_refs)


def _w_spec(w, layer, block, index_map):
    if w.ndim == 2:
        return pl.BlockSpec(block, index_map)
    return pl.BlockSpec((None,) + block, lambda *idx: (layer,) + index_map(*idx))


def matmul(a_list, w, *, col_off, n_cols, tn, epilogue, extras=(), outs, tm=TM, name, layer=0):
    m = a_list[0].shape[0]
    tm = min(tm, m)
    k_total = w.shape[-2]
    assert sum(a.shape[1] for a in a_list) == k_total
    assert col_off % tn == 0 and n_cols % tn == 0 and m % tm == 0
    off_blocks = col_off // tn
    in_specs = [pl.BlockSpec((tm, a.shape[1]), lambda i, j: (i, 0)) for a in a_list]
    in_specs.append(_w_spec(w, layer, (k_total, tn), lambda i, j: (0, j + off_blocks)))
    in_specs += [pl.BlockSpec(bs, im) for (_, bs, im) in extras]
    out_specs = [pl.BlockSpec(bs, im) for (_, _, bs, im) in outs]
    out_shape = [jax.ShapeDtypeStruct(s, dt) for (s, dt, _, _) in outs]
    res = pl.pallas_call(
        functools.partial(_mm_kernel, n_a=len(a_list), n_extra=len(extras), epilogue=epilogue),
        grid=(m // tm, n_cols // tn),
        in_specs=in_specs,
        out_specs=out_specs,
        out_shape=out_shape,
        compiler_params=_cparams("parallel", "arbitrary"),
        name=name,
    )(*a_list, w, *[e[0] for e in extras])
    return res


def _epi_scale(acc, extra_refs, out_refs, *, scale):
    out_refs[0][...] = (acc * scale if scale != 1.0 else acc).astype(out_refs[0].dtype)


def _epi_silu(acc, extra_refs, out_refs):
    out_refs[0][...] = _silu(acc).astype(out_refs[0].dtype)


def _epi_residual(acc, extra_refs, out_refs):
    out_refs[0][...] = extra_refs[0][...] + acc


def _epi_rotary(acc, extra_refs, out_refs):
    cos = extra_refs[0][...]
    sin = extra_refs[1][...]
    half = RET_DIM // 2
    tn = acc.shape[1]
    is_q = pl.program_id(1) * tn < RET_WIDTH
    scale = jnp.where(is_q, 1.0, RET_DIM ** -0.5).astype(F32)
    for hd in range(tn // RET_DIM):
        lo = hd * RET_DIM
        x1 = acc[:, lo:lo + half]
        x2 = acc[:, lo + half:lo + RET_DIM]
        out_refs[0][:, lo:lo + half] = ((x1 * cos - x2 * sin) * scale).astype(BF16)
        out_refs[0][:, lo + half:lo + RET_DIM] = ((x1 * sin + x2 * cos) * scale).astype(BF16)


def _epi_segments(acc, extra_refs, out_refs, *, segments):
    j = pl.program_id(1)
    for lo, hi, epi in segments:
        pl.when((j >= lo) & (j < hi))(functools.partial(epi, acc, extra_refs, out_refs))


def _epi_log_sigmoid(acc, extra_refs, out_refs):
    z = acc + extra_refs[0][...]
    out_refs[0][...] = jnp.minimum(z, 0.0) - jnp.log1p(jnp.exp(-jnp.abs(z)))


def _epi_softplus(acc, extra_refs, out_refs):
    out_refs[0][...] = _softplus(acc + extra_refs[0][...])


def _simple_out(m, n, tm, tn, dtype):
    tm = min(tm, m)
    return [((m, n), dtype, (tm, tn), lambda i, j: (i, j))]


def _gateup_kernel(h_ref, wg_ref, wu_ref, o_ref):
    h = h_ref[...]
    g = _dot(h, wg_ref[...])
    u = _dot(h, wu_ref[...])
    o_ref[...] = (_silu(g) * u).astype(o_ref.dtype)


def gateup(h, wg, wu, layer, tn=512):
    m, k = h.shape
    n = wg.shape[-1]
    tm = min(TM, m)
    return pl.pallas_call(
        _gateup_kernel,
        grid=(m // tm, n // tn),
        in_specs=[pl.BlockSpec((tm, k), lambda i, j: (i, 0)),
                  _w_spec(wg, layer, (k, tn), lambda i, j: (0, j)),
                  _w_spec(wu, layer, (k, tn), lambda i, j: (0, j))],
        out_specs=pl.BlockSpec((tm, tn), lambda i, j: (i, j)),
        out_shape=jax.ShapeDtypeStruct((m, n), BF16),
        compiler_params=_cparams("parallel", "arbitrary"),
        name="ffn_gateup",
    )(h, wg, wu)


def _mm_k_residual_kernel(a_ref, w_ref, x_ref, o_ref, acc_ref):
    kk = pl.program_id(2)
    part = _dot(a_ref[...], w_ref[...])

    @pl.when(kk == 0)
    def _():
        acc_ref[...] = part

    @pl.when(kk > 0)
    def _():
        acc_ref[...] += part

    @pl.when(kk == pl.num_programs(2) - 1)
    def _():
        o_ref[...] = x_ref[...] + acc_ref[...]


def matmul_k_residual(a, w, x, *, tk, tn=1024, name, layer=0):
    m, k = a.shape
    n = w.shape[-1]
    tm = min(TM, m)
    assert w.shape[-2] == k and k % tk == 0 and n % tn == 0
    return pl.pallas_call(
        _mm_k_residual_kernel,
        grid=(m // tm, n // tn, k // tk),
        in_specs=[pl.BlockSpec((tm, tk), lambda i, j, kk: (i, kk)),
                  _w_spec(w, layer, (tk, tn), lambda i, j, kk: (kk, j)),
                  pl.BlockSpec((tm, tn), lambda i, j, kk: (i, j))],
        out_specs=pl.BlockSpec((tm, tn), lambda i, j, kk: (i, j)),
        out_shape=jax.ShapeDtypeStruct((m, n), F32),
        scratch_shapes=[pltpu.VMEM((tm, tn), F32)],
        compiler_params=_cparams("parallel", "parallel", "arbitrary"),
        name=name,
    )(a, w, x)


def _cumsum_kernel(x_ref, o_ref, carry_ref):
    @pl.when(pl.program_id(0) == 0)
    def _():
        carry_ref[...] = jnp.zeros_like(carry_ref)

    n = x_ref.shape[0]
    tri = _lower_tri(n).astype(F32)
    c = _dot_f32(tri, x_ref[...]) + carry_ref[0:1, :]
    o_ref[...] = c
    carry_ref[...] = jnp.broadcast_to(c[n - 1:n, :], carry_ref.shape)


def cumsum_rows(x):
    m, n = x.shape
    tb = min(CUMSUM_BLOCK, m)
    return pl.pallas_call(
        _cumsum_kernel,
        grid=(m // tb,),
        in_specs=[pl.BlockSpec((tb, n), lambda i: (i, 0))],
        out_specs=pl.BlockSpec((tb, n), lambda i: (i, 0)),
        out_shape=jax.ShapeDtypeStruct((m, n), F32),
        scratch_shapes=[pltpu.VMEM((SUBLANES, n), F32)],
        compiler_params=_cparams("arbitrary"),
        name="cumsum_rows",
    )(x)


def _retention_kernel(lg_ref, q_ref, k_ref, v_ref, g_ref, o_ref, state_ref, *, chunk, n_sub):
    head = pl.program_id(0)

    @pl.when(pl.program_id(1) == 0)
    def _():
        state_ref[...] = jnp.zeros_like(state_ref)

    lg = lg_ref[head]
    L = chunk
    ti = lax.broadcasted_iota(jnp.int32, (L, L), 0)
    si = lax.broadcasted_iota(jnp.int32, (L, L), 1)
    causal = ti >= si
    diff = jnp.where(causal, (ti - si).astype(F32), 0.0)
    decay_in = jnp.where(causal, jnp.exp(lg * diff), 0.0)
    idx = lax.broadcasted_iota(jnp.int32, (L, 1), 0).astype(F32)
    q_decay = jnp.exp(lg * (idx + 1.0))
    k_decay = jnp.exp(lg * (L - 1.0 - idx))
    chunk_decay = jnp.exp(lg * float(L))

    for u in range(n_sub):
        rows = slice(u * L, (u + 1) * L)
        q = q_ref[rows, :]
        k = k_ref[rows, :]
        v = v_ref[rows, :]
        state = state_ref[...]
        scores = _dot_nt(q, k) * decay_in
        inner = _dot(scores.astype(BF16), v)
        cross = _dot(q, state.astype(BF16)) * q_decay
        out = inner + cross
        kd = (k.astype(F32) * k_decay).astype(BF16)
        state_ref[...] = state * chunk_decay + _dot_tn(kd, v)
        ms = jnp.mean(out * out, axis=-1, keepdims=True)
        o_ref[rows, :] = (out * lax.rsqrt(ms + EPS) * g_ref[rows, :].astype(F32)).astype(o_ref.dtype)


def retention(proj, log_gamma):
    s = proj.shape[0]
    chunk = min(RET_CHUNK, s)
    tb = min(RET_BLOCK, s)
    return pl.pallas_call(
        functools.partial(_retention_kernel, chunk=chunk, n_sub=tb // chunk),
        grid=(RET_HEADS, s // tb),
        in_specs=[pl.BlockSpec(memory_space=pltpu.SMEM),
                  pl.BlockSpec((tb, RET_DIM), lambda h, c: (c, h)),
                  pl.BlockSpec((tb, RET_DIM), lambda h, c: (c, RET_HEADS + h)),
                  pl.BlockSpec((tb, RET_DIM), lambda h, c: (c, 2 * RET_HEADS + h)),
                  pl.BlockSpec((tb, RET_DIM), lambda h, c: (c, 3 * RET_HEADS + h))],
        out_specs=pl.BlockSpec((tb, RET_DIM), lambda h, c: (c, h)),
        out_shape=jax.ShapeDtypeStruct((s, RET_WIDTH), BF16),
        scratch_shapes=[pltpu.VMEM((RET_DIM, RET_DIM), F32)],
        compiler_params=_cparams("parallel", "arbitrary"),
        name="retention",
    )(log_gamma, proj, proj, proj, proj)


def _split3(x):
    hi = x.astype(BF16)
    r1 = x - hi.astype(F32)
    mid = r1.astype(BF16)
    lo = (r1 - mid.astype(F32)).astype(BF16)
    return hi, mid, lo


def _fox_prep_kernel(q_ref, k_ref, v_ref, c_ref, qa_ref, ka_ref, va_ref):
    tb = q_ref.shape[0]
    d = FOX_HEAD_DIM
    lane = lax.broadcasted_iota(jnp.int32, (tb, d), 1)
    one = jnp.ones((tb, d), F32)
    zero = jnp.zeros((tb, d), F32)
    c_all = c_ref[...] * LOG2E
    for hd in range(FOX_HEADS):
        hi, mid, lo = [jnp.broadcast_to(part.astype(F32), (tb, d)) for part in _split3(c_all[:, hd:hd + 1])]
        aug_q = jnp.where(lane == 0, hi, jnp.where(lane == 1, mid, jnp.where(lane == 2, lo,
                          jnp.where(lane < 6, one, zero))))
        aug_k = jnp.where(lane < 3, one, jnp.where(lane == 3, -hi, jnp.where(lane == 4, -mid,
                          jnp.where(lane == 5, -lo, zero))))
        src = slice(hd * d, (hd + 1) * d)
        data = slice(2 * hd * d, (2 * hd + 1) * d)
        aug = slice((2 * hd + 1) * d, (2 * hd + 2) * d)
        qa_ref[:, data] = q_ref[:, src]
        qa_ref[:, aug] = aug_q.astype(BF16)
        ka_ref[:, data] = k_ref[:, src]
        ka_ref[:, aug] = aug_k.astype(BF16)
        va_ref[:, data] = v_ref[:, src]
        va_ref[:, aug] = one.astype(BF16)


def fox_prep(proj, c, tb=512):
    s = proj.shape[0]
    tb = min(tb, s)
    w = FOX_WIDTH
    first = 4 * RET_WIDTH // w
    out = jax.ShapeDtypeStruct((s, 2 * w), BF16)
    return pl.pallas_call(
        _fox_prep_kernel,
        grid=(s // tb,),
        in_specs=[pl.BlockSpec((tb, w), lambda i: (i, first)),
                  pl.BlockSpec((tb, w), lambda i: (i, first + 1)),
                  pl.BlockSpec((tb, w), lambda i: (i, first + 2)),
                  pl.BlockSpec((tb, LANES), lambda i: (i, 0))],
        out_specs=[pl.BlockSpec((tb, 2 * w), lambda i: (i, 0))] * 3,
        out_shape=[out, out, out],
        compiler_params=_cparams("parallel"),
        name="fox_prep",
    )(proj, proj, proj, c)


def _fox_kernel(q_ref, k_ref, v_ref, o_ref, s0_ref, s1_ref, m_ref, acc_ref, *, blk, sub):
    qi = pl.program_id(1)
    d = FOX_HEAD_DIM
    q = q_ref[...]

    def scores(jsub):
        start = pl.multiple_of(jsub * sub, sub)
        return _dot_nt(q, k_ref[pl.ds(start, sub), :])

    def process(s_ref, jsub, mask_off):
        start = pl.multiple_of(jsub * sub, sub)
        s = s_ref[...]
        if mask_off is not None:
            r = lax.broadcasted_iota(jnp.int32, (blk, sub), 0)
            c = lax.broadcasted_iota(jnp.int32, (blk, sub), 1)
            s = jnp.where(r >= c + mask_off, s, NEG_BIG)
        m_old = m_ref[...]
        m_new = jnp.maximum(m_old, jnp.max(s, axis=1, keepdims=True))
        alpha = jnp.exp2(m_old - m_new)
        p = jnp.concatenate([jnp.exp2(s[:, u * d:(u + 1) * d] - m_new) for u in range(sub // d)], axis=1)
        pv = _dot(p.astype(BF16), v_ref[pl.ds(start, sub), :])
        acc_ref[:, :d] = alpha * acc_ref[:, :d] + pv[:, :d]
        acc_ref[:, d:] = alpha * acc_ref[:, d:] + pv[:, d:]
        m_ref[...] = m_new

    m_ref[...] = jnp.full_like(m_ref, NEG_BIG)
    acc_ref[...] = jnp.zeros_like(acc_ref)
    s0_ref[...] = scores(0)

    def body(t, carry):
        s1_ref[...] = scores(2 * t + 1)
        process(s0_ref, 2 * t, None)
        s0_ref[...] = scores(2 * t + 2)
        process(s1_ref, 2 * t + 1, None)
        return carry

    lax.fori_loop(0, qi, body, 0)
    s1_ref[...] = scores(2 * qi + 1)
    process(s0_ref, 2 * qi, 0)
    process(s1_ref, 2 * qi + 1, sub)
    o_ref[...] = (acc_ref[:, :d] / acc_ref[:, d:]).astype(o_ref.dtype)


def fox_attention(qa, ka, va):
    s = qa.shape[0]
    blk = min(FOX_BLOCK, s)
    sub = blk // 2
    d = FOX_HEAD_DIM
    return pl.pallas_call(
        functools.partial(_fox_kernel, blk=blk, sub=sub),
        grid=(FOX_HEADS, s // blk),
        in_specs=[pl.BlockSpec((blk, 2 * d), lambda h, i: (i, h)),
                  pl.BlockSpec((s, 2 * d), lambda h, i: (0, h)),
                  pl.BlockSpec((s, 2 * d), lambda h, i: (0, h))],
        out_specs=pl.BlockSpec((blk, d), lambda h, i: (i, h)),
        out_shape=jax.ShapeDtypeStruct((s, FOX_WIDTH), BF16),
        scratch_shapes=[pltpu.VMEM((blk, sub), F32), pltpu.VMEM((blk, sub), F32),
                        pltpu.VMEM((blk, d), F32), pltpu.VMEM((blk, 2 * d), F32)],
        compiler_params=_cparams("parallel", "arbitrary"),
        name="fox_attention",
    )(qa, ka, va)


def _rms_scale(x, g):
    return x * lax.rsqrt(jnp.mean(x * x, axis=-1, keepdims=True) + EPS) * g


def _xattn_kernel(x_ref, g_in_ref, g_out_ref, wq_ref, kt_ref, v_ref, wo_ref, o_ref, h_out_ref):
    x = x_ref[...]
    h = _rms_scale(x, g_in_ref[...]).astype(BF16)
    q = _dot(h, wq_ref[...]).astype(BF16)
    heads = []
    for hd in range(XA_HEADS):
        cols = slice(hd * XA_HEAD_DIM, (hd + 1) * XA_HEAD_DIM)
        logits = _dot(q[:, cols], kt_ref[hd]) * (XA_HEAD_DIM ** -0.5)
        mx = jnp.max(logits, axis=-1, keepdims=True)
        e = jnp.exp(logits - mx)
        p = e / jnp.sum(e, axis=-1, keepdims=True)
        heads.append(_dot(p.astype(BF16), v_ref[:, cols]))
    o = jnp.concatenate(heads, axis=-1).astype(BF16)
    x_new = x + _dot(o, wo_ref[...])
    o_ref[...] = x_new
    h_out_ref[...] = _rms_scale(x_new, g_out_ref[...]).astype(BF16)


def cross_attention(x, g_in, g_out, wq, k_t, v, wo, layer, tm=256):
    s, d = x.shape
    tm = min(tm, s)
    mem_len = v.shape[0]
    const2 = lambda i: (0, 0)
    row = lambda i: (i, 0)
    return pl.pallas_call(
        _xattn_kernel,
        grid=(s // tm,),
        in_specs=[pl.BlockSpec((tm, d), row),
                  pl.BlockSpec((1, d), const2),
                  pl.BlockSpec((1, d), const2),
                  _w_spec(wq, layer, (d, XA_WIDTH), const2),
                  pl.BlockSpec((XA_HEADS, XA_HEAD_DIM, mem_len), lambda i: (0, 0, 0)),
                  pl.BlockSpec((mem_len, XA_WIDTH), const2),
                  _w_spec(wo, layer, (XA_WIDTH, d), const2)],
        out_specs=[pl.BlockSpec((tm, d), row), pl.BlockSpec((tm, d), row)],
        out_shape=[jax.ShapeDtypeStruct((s, d), F32), jax.ShapeDtypeStruct((s, d), BF16)],
        compiler_params=_cparams("parallel"),
        name="cross_attention",
    )(x, g_in.astype(F32).reshape(1, d), g_out.astype(F32).reshape(1, d), wq, k_t, v, wo)


SSD_HALO = 128
SSD_REP = 3
SSD_REP_LANES = LANES
SSD_GROUPS_PER_STEP = 1


def _ssd_constants(L):
    rows = jnp.arange((SSM_CONV - 1) * L)
    src = SSD_HALO + rows % L - (rows // L + 1)
    shift = (jnp.arange(L + SSD_HALO)[None, :] == src[:, None]).astype(BF16)
    lane = jnp.arange(SSD_REP_LANES)
    part_head = jnp.where(lane < SSD_REP * SSM_HPG, lane % SSM_HPG, -1)
    e_col = (part_head[:, None] == (jnp.arange(SSM_HPG * L) // L)[None, :]).astype(BF16)
    e_head = (part_head[:, None] == (jnp.arange(SSM_GROUP_WIDTH) // SSM_HEAD_DIM)[None, :]).astype(BF16)
    return shift, e_col, e_head


def _replicate_heads(x):
    x3 = jnp.tile(x, (1,) * (x.ndim - 1) + (SSD_REP,))
    return jnp.pad(x3, [(0, 0)] * (x.ndim - 1) + [(0, SSD_REP_LANES - SSD_REP * SSM_HPG)])


def _pack_split3(x3):
    lane = lax.broadcasted_iota(jnp.int32, x3.shape, 1)
    hi = x3.astype(BF16).astype(F32)
    r1 = x3 - hi
    mid = r1.astype(BF16).astype(F32)
    lo = r1 - mid
    parts = jnp.where(lane < SSM_HPG, hi, jnp.where(lane < 2 * SSM_HPG, mid, lo))
    return jnp.where(lane < SSD_REP * SSM_HPG, parts, 0.0).astype(BF16)


def _ssd_kernel(x_ref, b_ref, c_ref, wx_ref, wb_ref, wc_ref, bx_ref, bb_ref, bc_ref,
                shift_ref, ecol_ref, ehead_ref, dt_ref, dtt_ref, a_ref, at_ref, dskip_ref, z_ref, gnw_ref,
                y_ref, state_ref, *halo_refs, L):
    gw = SSM_GROUP_WIDTH
    n = SSM_STATE

    @pl.when(pl.program_id(1) == 0)
    def _():
        state_ref[...] = jnp.zeros_like(state_ref)
        for halo_ref in halo_refs:
            halo_ref[...] = jnp.zeros_like(halo_ref)

    causal = _lower_tri(L)
    tri = causal.astype(F32)
    first_half = lax.broadcasted_iota(jnp.int32, (L, LANES), 1) < SSM_HEAD_DIM

    for u in range(SSD_GROUPS_PER_STEP):
        wide_cols = slice(u * gw, (u + 1) * gw)
        state_cols = slice(u * n, (u + 1) * n)

        cur16 = jnp.concatenate([x_ref[:, wide_cols], b_ref[:, state_cols], c_ref[:, state_cols]], axis=1)
        buf = jnp.concatenate([halo_refs[u][...], cur16], axis=0)
        shifted = _dot(shift_ref[...], buf)
        cur = cur16.astype(F32)
        halo_refs[u][...] = cur16[L - SSD_HALO:, :]

        def conv_silu(cols, w, bias):
            y = bias + w[SSM_CONV - 1:SSM_CONV, :] * cur[:, cols]
            for back in range(1, SSM_CONV):
                y = y + w[SSM_CONV - 1 - back:SSM_CONV - back, :] * shifted[(back - 1) * L:back * L, cols]
            return _silu(y)

        xs = conv_silu(slice(0, gw), wx_ref[:, wide_cols], bx_ref[:, wide_cols])
        bm16 = conv_silu(slice(gw, gw + n), wb_ref[:, state_cols], bb_ref[:, state_cols]).astype(BF16)
        cm16 = conv_silu(slice(gw + n, gw + 2 * n), wc_ref[:, state_cols], bc_ref[:, state_cols]).astype(BF16)

        dt3 = dt_ref[u]
        acs3 = _dot_f32(tri, dt3 * a_ref[u])
        acs_t = lax.dot_general(dtt_ref[u] * at_ref[u], tri, (((1,), (1,)), ((), ())),
                                preferred_element_type=F32, precision=lax.Precision.HIGHEST)
        acs_last = acs3[L - 1:L, :]
        lhs = jnp.concatenate([_pack_split3(dt3), _pack_split3(jnp.exp2(acs3)),
                               _pack_split3(dt3 * jnp.exp2(acs_last - acs3))], axis=0)
        wide = _dot(lhs, ehead_ref[...])
        dt_w = wide[0:L, :]
        exp_acs_w = wide[L:2 * L, :]
        dt_tail_w = wide[2 * L:, :]
        acs_col = _dot(_pack_split3(acs3), ecol_ref[...])

        cb = _dot_nt(cm16, bm16)
        state = state_ref[u]
        y_off = _dot(cm16, state.astype(BF16))

        xdt16 = (xs * dt_w).astype(BF16)
        y_parts = []
        for pr in range(SSM_HPG // 2):
            e0, e1 = 2 * pr, 2 * pr + 1
            m0 = jnp.exp2(jnp.where(causal, acs_col[:, e0 * L:(e0 + 1) * L] - acs_t[e0:e0 + 1, :], NEG_BIG)) * cb
            m1 = jnp.exp2(jnp.where(causal, acs_col[:, e1 * L:(e1 + 1) * L] - acs_t[e1:e1 + 1, :], NEG_BIG)) * cb
            mcat = jnp.concatenate([m0, m1], axis=1).astype(BF16)
            pair16 = xdt16[:, pr * LANES:(pr + 1) * LANES]
            zero = jnp.zeros_like(pair16)
            bd = jnp.concatenate([jnp.where(first_half, pair16, zero),
                                  jnp.where(first_half, zero, pair16)], axis=0)
            y_parts.append(_dot(mcat, bd))
        y = jnp.concatenate(y_parts, axis=1) + y_off * exp_acs_w
        state_ref[u] = state * exp_acs_w[L - 1:L, :] + _dot_tn(bm16, (xs * dt_tail_w).astype(BF16))

        y = (y + xs * dskip_ref[:, wide_cols]) * z_ref[:, wide_cols].astype(F32)
        ms = jnp.mean(y * y, axis=-1, keepdims=True)
        y_ref[:, wide_cols] = (y * lax.rsqrt(ms + EPS) * gnw_ref[:, wide_cols]).astype(y_ref.dtype)


def ssd_mixer(proj, conv_w, conv_b, dt_g, dt_gt, a_g, a_gt, dskip_row, gn_w):
    s = proj.shape[0]
    L = min(SSD_CHUNK, s)
    p = SSD_GROUPS_PER_STEP
    gw = p * SSM_GROUP_WIDTH
    n = p * SSM_STATE
    xb = SSM_INNER // n
    cbk = xb + SSM_GROUPS // p
    pz = SSM_INNER // gw
    pb = SSM_INNER // n
    rep = SSD_REP_LANES
    shift, e_col, e_head = _ssd_constants(L)
    return pl.pallas_call(
        functools.partial(_ssd_kernel, L=L),
        grid=(SSM_GROUPS // p, s // L),
        in_specs=[pl.BlockSpec((L, gw), lambda g, c: (c, pz + g)),
                  pl.BlockSpec((L, n), lambda g, c: (c, pb + xb + g)),
                  pl.BlockSpec((L, n), lambda g, c: (c, pb + cbk + g)),
                  pl.BlockSpec((SSM_CONV, gw), lambda g, c: (0, g)),
                  pl.BlockSpec((SSM_CONV, n), lambda g, c: (0, xb + g)),
                  pl.BlockSpec((SSM_CONV, n), lambda g, c: (0, cbk + g)),
                  pl.BlockSpec((1, gw), lambda g, c: (0, g)),
                  pl.BlockSpec((1, n), lambda g, c: (0, xb + g)),
                  pl.BlockSpec((1, n), lambda g, c: (0, cbk + g)),
                  pl.BlockSpec(shift.shape, lambda g, c: (0, 0)),
                  pl.BlockSpec(e_col.shape, lambda g, c: (0, 0)),
                  pl.BlockSpec(e_head.shape, lambda g, c: (0, 0)),
                  pl.BlockSpec((p, L, rep), lambda g, c: (g, c, 0)),
                  pl.BlockSpec((p, SSM_HPG, L), lambda g, c: (g, 0, c)),
                  pl.BlockSpec((p, 1, rep), lambda g, c: (g, 0, 0)),
                  pl.BlockSpec((p, SSM_HPG, 1), lambda g, c: (g, 0, 0)),
                  pl.BlockSpec((1, gw), lambda g, c: (0, g)),
                  pl.BlockSpec((L, gw), lambda g, c: (c, g)),
                  pl.BlockSpec((1, gw), lambda g, c: (0, g))],
        out_specs=pl.BlockSpec((L, gw), lambda g, c: (c, g)),
        out_shape=jax.ShapeDtypeStruct((s, SSM_INNER), BF16),
        scratch_shapes=[pltpu.VMEM((p, SSM_STATE, SSM_GROUP_WIDTH), F32)]
        + [pltpu.VMEM((SSD_HALO, SSM_GROUP_WIDTH + 2 * SSM_STATE), BF16)] * p,
        compiler_params=_cparams("parallel", "arbitrary"),
        name="ssd_mixer",
    )(proj, proj, proj, conv_w, conv_w, conv_w, conv_b, conv_b, conv_b, shift, e_col, e_head,
      _replicate_heads(dt_g), dt_gt, _replicate_heads(a_g), a_gt, dskip_row, proj, gn_w)


PROJ_TN = 512
EV_PROJ_COLS = 4 * RET_WIDTH + 3 * FOX_WIDTH
OD_PROJ_COLS = SSM_INNER + SSM_CONV_DIM


def _retention_fox_layer(x, norm_g, w_in16, i, b_f, w_out16):
    s = x.shape[0]
    tm = min(TM, s)
    h = rmsnorm(x, norm_g, BF16)

    half = RET_DIM // 2
    inv = ROPE_BASE ** (-jnp.arange(half, dtype=F32) / half)
    ang = jnp.arange(s).astype(F32)[:, None] * inv[None, :]
    cos, sin = jnp.cos(ang), jnp.sin(ang)

    plain = functools.partial(_epi_scale, scale=1.0)
    fq_scale = functools.partial(_epi_scale, scale=LOG2E * FOX_HEAD_DIM ** -0.5)
    bounds = [0, 2 * RET_WIDTH, 3 * RET_WIDTH, 4 * RET_WIDTH, 4 * RET_WIDTH + FOX_WIDTH, EV_PROJ_COLS]
    epis = [_epi_rotary, plain, _epi_silu, fq_scale, plain]
    segments = [(lo // PROJ_TN, hi // PROJ_TN, e) for lo, hi, e in zip(bounds[:-1], bounds[1:], epis)]
    (proj,) = matmul([h], w_in16, layer=i, col_off=0, n_cols=EV_PROJ_COLS, tn=PROJ_TN,
                     epilogue=functools.partial(_epi_segments, segments=segments),
                     extras=[(cos, (tm, half), lambda i_, j: (i_, 0)), (sin, (tm, half), lambda i_, j: (i_, 0))],
                     outs=_simple_out(s, EV_PROJ_COLS, tm, PROJ_TN, BF16), name="ev_proj")

    w_ff = jnp.pad(w_in16[i, :, EV_PROJ_COLS:EV_PROJ_COLS + FOX_HEADS], ((0, 0), (0, LANES - FOX_HEADS)))
    b_pad = jnp.pad(b_f.astype(F32), (0, LANES - FOX_HEADS)).reshape(1, LANES)
    (log_f,) = matmul([h], w_ff, col_off=0, n_cols=LANES, tn=LANES, epilogue=_epi_log_sigmoid,
                      extras=[(b_pad, (1, LANES), lambda i_, j: (0, 0))],
                      outs=_simple_out(s, LANES, tm, LANES, F32), name="ev_proj_forget")
    c = cumsum_rows(log_f)

    log_gamma = jnp.log1p(-jnp.exp2(-5.0 - jnp.arange(RET_HEADS, dtype=F32)))
    ret = retention(proj, log_gamma)
    fox = fox_attention(*fox_prep(proj, c))

    (x_new,) = matmul([ret, fox], w_out16, layer=i, col_off=0, n_cols=D_MODEL, tn=1024, epilogue=_epi_residual,
                      extras=[(x, (tm, 1024), lambda i_, j: (i_, j))],
                      outs=_simple_out(s, D_MODEL, tm, 1024, F32), name="ev_out_proj")
    return x_new


def _mamba2_layer(x, norm_g, w_in16, i, conv_w, conv_b, dt_bias, a_log, d_skip, gn_w, w_out16):
    s = x.shape[0]
    tm = min(TM, s)
    h = rmsnorm(x, norm_g, BF16)
    segments = [(0, SSM_INNER // PROJ_TN, _epi_silu),
                (SSM_INNER // PROJ_TN, OD_PROJ_COLS // PROJ_TN, functools.partial(_epi_scale, scale=1.0))]
    (proj,) = matmul([h], w_in16, layer=i, col_off=0, n_cols=OD_PROJ_COLS, tn=PROJ_TN,
                     epilogue=functools.partial(_epi_segments, segments=segments),
                     outs=_simple_out(s, OD_PROJ_COLS, tm, PROJ_TN, BF16), name="od_proj")
    (dt,) = matmul([h], w_in16, layer=i, col_off=OD_PROJ_COLS, n_cols=SSM_HEADS, tn=SSM_HEADS,
                   epilogue=_epi_softplus,
                   extras=[(dt_bias.astype(F32).reshape(1, SSM_HEADS), (1, SSM_HEADS), lambda i_, j: (0, 0))],
                   outs=_simple_out(s, SSM_HEADS, tm, SSM_HEADS, F32), name="od_proj_dt")

    dt_g = dt.reshape(s, SSM_GROUPS, SSM_HPG).transpose(1, 0, 2)
    dt_gt = dt_g.transpose(0, 2, 1)
    a = -jnp.exp(a_log.astype(F32)) * LOG2E
    a_g = a.reshape(SSM_GROUPS, 1, SSM_HPG)
    a_gt = a.reshape(SSM_GROUPS, SSM_HPG, 1)
    dskip_row = jnp.repeat(d_skip.astype(F32), SSM_HEAD_DIM).reshape(1, SSM_INNER)
    y = ssd_mixer(proj, conv_w.astype(F32), conv_b.astype(F32).reshape(1, SSM_CONV_DIM),
                  dt_g, dt_gt, a_g, a_gt, dskip_row, gn_w.astype(F32).reshape(1, SSM_INNER))
    return matmul_k_residual(y, w_out16, x, layer=i, tk=2048, name="od_out_proj")


def _cross_attention_layer(x, g_in, g_out, mem_n, wq16, wk16, wv16, wo16, layer):
    m = mem_n.shape[0]
    plain = functools.partial(_epi_scale, scale=1.0)
    (k,) = matmul([mem_n], wk16, layer=layer, col_off=0, n_cols=XA_WIDTH, tn=XA_WIDTH, epilogue=plain,
                  outs=_simple_out(m, XA_WIDTH, m, XA_WIDTH, BF16), name="xa_proj_k")
    (v,) = matmul([mem_n], wv16, layer=layer, col_off=0, n_cols=XA_WIDTH, tn=XA_WIDTH, epilogue=plain,
                  outs=_simple_out(m, XA_WIDTH, m, XA_WIDTH, BF16), name="xa_proj_v")
    k_t = k.reshape(m, XA_HEADS, XA_HEAD_DIM).transpose(1, 2, 0)
    return cross_attention(x, g_in, g_out, wq16, k_t, v, wo16, layer)


def kernel(x, mem, mem_norm, ev_mix_norm, ev_w_in, ev_b_f, ev_w_out, od_mix_norm, od_w_in, od_conv_w, od_conv_b,
           od_dt_bias, od_a_log, od_d_skip, od_gn_w, od_w_out, xa_norm, xa_wq, xa_wk, xa_wv, xa_wo, ffn_norm,
           ffn_w_gate, ffn_w_up, ffn_w_down, final_norm):
    b, s, d = x.shape
    assert b == 1 and d == D_MODEL
    xs = x.reshape(s, d).astype(F32)
    mem_n = rmsnorm(mem.reshape(mem.shape[1], d).astype(F32), mem_norm, BF16)

    ev_w_in16, ev_w_out16 = cast_pad_bf16(ev_w_in), ev_w_out.astype(BF16)
    od_w_in16, od_w_out16 = od_w_in.astype(BF16), od_w_out.astype(BF16)
    xa_wq16, xa_wk16, xa_wv16, xa_wo16 = [w.astype(BF16) for w in (xa_wq, xa_wk, xa_wv, xa_wo)]
    wg16 = cast_pad_bf16(ffn_w_gate, cols_out=D_FF_PAD)
    wu16 = cast_pad_bf16(ffn_w_up, cols_out=D_FF_PAD)
    wd16 = cast_pad_bf16(ffn_w_down, rows_out=D_FF_PAD)

    depth = xa_norm.shape[0]
    for layer in range(depth):
        i = layer // 2
        if layer % 2 == 0:
            xs = _retention_fox_layer(xs, ev_mix_norm[i], ev_w_in16, i, ev_b_f[i], ev_w_out16)
        else:
            xs = _mamba2_layer(xs, od_mix_norm[i], od_w_in16, i, od_conv_w[i], od_conv_b[i], od_dt_bias[i],
                               od_a_log[i], od_d_skip[i], od_gn_w[i], od_w_out16)
        xs, h_ffn = _cross_attention_layer(xs, xa_norm[layer], ffn_norm[layer], mem_n,
                                           xa_wq16, xa_wk16, xa_wv16, xa_wo16, layer)
        a = gateup(h_ffn, wg16, wu16, layer)
        xs = matmul_k_residual(a, wd16, xs, layer=layer, tk=DOWN_TK, name="ffn_down")
    out = rmsnorm(xs, final_norm, F32)
    return out.reshape(b, s, d)
```

```python
import functools
import math

import jax
import jax.numpy as jnp
from jax import lax
from jax.experimental import pallas as pl
from jax.experimental.pallas import tpu as pltpu

F32 = jnp.float32
BF16 = jnp.bfloat16

D_MODEL = 4096
EPS = 1e-6
RET_HEADS = 8
RET_DIM = 256
RET_WIDTH = RET_HEADS * RET_DIM
ROPE_BASE = 10000.0
FOX_HEAD_DIM = 128
FOX_WIDTH = 2048
FOX_HEADS = FOX_WIDTH // FOX_HEAD_DIM
SSM_INNER = 8192
SSM_HEAD_DIM = 64
SSM_HEADS = SSM_INNER // SSM_HEAD_DIM
SSM_STATE = 128
SSM_GROUPS = 8
SSM_HPG = SSM_HEADS // SSM_GROUPS
SSM_GROUP_WIDTH = SSM_INNER // SSM_GROUPS
SSM_CONV = 4
SSM_CONV_DIM = SSM_INNER + 2 * SSM_GROUPS * SSM_STATE
XA_HEADS = 4
XA_HEAD_DIM = 128
XA_WIDTH = XA_HEADS * XA_HEAD_DIM
D_FF = 11008

V7X_VMEM_BYTES = 64 * 1024 * 1024
VMEM_LIMIT_BYTES = V7X_VMEM_BYTES - 8 * 1024 * 1024
LANES = 128
SUBLANES = 8

TM = 1024
D_FF_PAD = 11264
DOWN_TK = 2816
RET_CHUNK = 256
RET_BLOCK = 512
FOX_BLOCK = 1024
SSD_CHUNK = 128
CUMSUM_BLOCK = 256
NEG_BIG = -1e30
LOG2E = 1.4426950408889634


def _cparams(*sem):
    return pltpu.CompilerParams(dimension_semantics=sem, vmem_limit_bytes=VMEM_LIMIT_BYTES)


def _silu(x):
    h = 0.5 * x
    return h + h * jnp.tanh(h)


def _softplus(x):
    return jnp.maximum(x, 0.0) + jnp.log1p(jnp.exp(-jnp.abs(x)))


def _dot(a, b):
    return jnp.dot(a, b, preferred_element_type=F32)


def _dot_nt(a, b):
    return lax.dot_general(a, b, (((1,), (1,)), ((), ())), preferred_element_type=F32)


def _dot_tn(a, b):
    return lax.dot_general(a, b, (((0,), (0,)), ((), ())), preferred_element_type=F32)


def _dot_f32(a, b):
    return jnp.dot(a, b, preferred_element_type=F32, precision=lax.Precision.HIGHEST)


def _lower_tri(n):
    r = lax.broadcasted_iota(jnp.int32, (n, n), 0)
    c = lax.broadcasted_iota(jnp.int32, (n, n), 1)
    return r >= c


def _rmsnorm_kernel(x_ref, g_ref, o_ref):
    x = x_ref[...]
    ms = jnp.mean(x * x, axis=-1, keepdims=True)
    o_ref[...] = (x * lax.rsqrt(ms + EPS) * g_ref[...]).astype(o_ref.dtype)


def rmsnorm(x, g, out_dtype, tm=512):
    m, d = x.shape
    tm = min(tm, m)
    return pl.pallas_call(
        _rmsnorm_kernel,
        grid=(m // tm,),
        in_specs=[pl.BlockSpec((tm, d), lambda i: (i, 0)),
                  pl.BlockSpec((1, d), lambda i: (0, 0))],
        out_specs=pl.BlockSpec((tm, d), lambda i: (i, 0)),
        out_shape=jax.ShapeDtypeStruct((m, d), out_dtype),
        compiler_params=_cparams("parallel"),
        name="rmsnorm",
    )(x, g.reshape(1, d).astype(F32))


def _cast_pad_kernel(x_ref, o_ref, *, n_row_blocks_in, cols_in):
    r = pl.program_id(1)
    rows, cols_out = o_ref.shape
    vals = jnp.where(r < n_row_blocks_in, x_ref[...], 0.0).astype(o_ref.dtype)
    o_ref[:, :cols_in] = vals
    if cols_out > cols_in:
        o_ref[:, cols_in:] = jnp.zeros((rows, cols_out - cols_in), o_ref.dtype)


def cast_pad_bf16(w, rows_out=None, cols_out=None, tr=256):
    layers, rows_in, cols_in = w.shape
    rows_out = rows_out or rows_in
    cols_out = cols_out or cols_in
    assert rows_in % tr == 0 and rows_out % tr == 0
    n_in = rows_in // tr
    return pl.pallas_call(
        functools.partial(_cast_pad_kernel, n_row_blocks_in=n_in, cols_in=cols_in),
        grid=(layers, rows_out // tr),
        in_specs=[pl.BlockSpec((None, tr, cols_in), lambda l, r: (l, jnp.minimum(r, n_in - 1), 0))],
        out_specs=pl.BlockSpec((None, tr, cols_out), lambda l, r: (l, r, 0)),
        out_shape=jax.ShapeDtypeStruct((layers, rows_out, cols_out), BF16),
        compiler_params=_cparams("parallel", "parallel"),
        name="cast_pad_bf16",
    )(w)


def _mm_kernel(*refs, n_a, n_extra, epilogue):
    a_refs = refs[:n_a]
    w_ref = refs[n_a]
    extra_refs = refs[n_a + 1:n_a + 1 + n_extra]
    out_refs = refs[n_a + 1 + n_extra:]
    acc = None
    off = 0
    for a_ref in a_refs:
        k = a_ref.shape[1]
        part = _dot(a_ref[...], w_ref[off:off + k, :])
        acc = part if acc is None else acc + part
        off += k
    epilogue(acc, extra_refs, out_refs)


def _w_spec(w, layer, block, index_map):
    if w.ndim == 2:
        return pl.BlockSpec(block, index_map)
    return pl.BlockSpec((None,) + block, lambda *idx: (layer,) + index_map(*idx))


def matmul(a_list, w, *, col_off, n_cols, tn, epilogue, extras=(), outs, tm=TM, name, layer=0):
    m = a_list[0].shape[0]
    tm = min(tm, m)
    k_total = w.shape[-2]
    assert sum(a.shape[1] for a in a_list) == k_total
    assert col_off % tn == 0 and n_cols % tn == 0 and m % tm == 0
    off_blocks = col_off // tn
    in_specs = [pl.BlockSpec((tm, a.shape[1]), lambda i, j: (i, 0)) for a in a_list]
    in_specs.append(_w_spec(w, layer, (k_total, tn), lambda i, j: (0, j + off_blocks)))
    in_specs += [pl.BlockSpec(bs, im) for (_, bs, im) in extras]
    out_specs = [pl.BlockSpec(bs, im) for (_, _, bs, im) in outs]
    out_shape = [jax.ShapeDtypeStruct(s, dt) for (s, dt, _, _) in outs]
    res = pl.pallas_call(
        functools.partial(_mm_kernel, n_a=len(a_list), n_extra=len(extras), epilogue=epilogue),
        grid=(m // tm, n_cols // tn),
        in_specs=in_specs,
        out_specs=out_specs,
        out_shape=out_shape,
        compiler_params=_cparams("parallel", "arbitrary"),
        name=name,
    )(*a_list, w, *[e[0] for e in extras])
    return res


def _epi_scale(acc, extra_refs, out_refs, *, scale):
    out_refs[0][...] = (acc * scale if scale != 1.0 else acc).astype(out_refs[0].dtype)


def _epi_silu(acc, extra_refs, out_refs):
    out_refs[0][...] = _silu(acc).astype(out_refs[0].dtype)


def _epi_residual(acc, extra_refs, out_refs):
    out_refs[0][...] = extra_refs[0][...] + acc


def _epi_rotary(acc, extra_refs, out_refs):
    cos = extra_refs[0][...]
    sin = extra_refs[1][...]
    half = RET_DIM // 2
    tn = acc.shape[1]
    is_q = pl.program_id(1) * tn < RET_WIDTH
    scale = jnp.where(is_q, 1.0, RET_DIM ** -0.5).astype(F32)
    for hd in range(tn // RET_DIM):
        lo = hd * RET_DIM
        x1 = acc[:, lo:lo + half]
        x2 = acc[:, lo + half:lo + RET_DIM]
        out_refs[0][:, lo:lo + half] = ((x1 * cos - x2 * sin) * scale).astype(BF16)
        out_refs[0][:, lo + half:lo + RET_DIM] = ((x1 * sin + x2 * cos) * scale).astype(BF16)


def _epi_segments(acc, extra_refs, out_refs, *, segments):
    j = pl.program_id(1)
    for lo, hi, epi in segments:
        pl.when((j >= lo) & (j < hi))(functools.partial(epi, acc, extra_refs, out_refs))


def _epi_od_proj(acc, extra_refs, out_refs):
    is_z = pl.program_id(1) * acc.shape[1] < SSM_INNER
    out_refs[0][...] = jnp.where(is_z, _silu(acc), acc).astype(BF16)


def _epi_log_sigmoid(acc, extra_refs, out_refs):
    z = acc + extra_refs[0][...]
    out_refs[0][...] = jnp.minimum(z, 0.0) - jnp.log1p(jnp.exp(-jnp.abs(z)))


def _epi_softplus(acc, extra_refs, out_refs):
    out_refs[0][...] = _softplus(acc + extra_refs[0][...])


def _simple_out(m, n, tm, tn, dtype):
    tm = min(tm, m)
    return [((m, n), dtype, (tm, tn), lambda i, j: (i, j))]


def _gateup_kernel(h_ref, wg_ref, wu_ref, o_ref):
    h = h_ref[...]
    g = _dot(h, wg_ref[...])
    u = _dot(h, wu_ref[...])
    o_ref[...] = (_silu(g) * u).astype(o_ref.dtype)


def gateup(h, wg, wu, layer, tn=512):
    m, k = h.shape
    n = wg.shape[-1]
    tm = min(TM, m)
    return pl.pallas_call(
        _gateup_kernel,
        grid=(m // tm, n // tn),
        in_specs=[pl.BlockSpec((tm, k), lambda i, j: (i, 0)),
                  _w_spec(wg, layer, (k, tn), lambda i, j: (0, j)),
                  _w_spec(wu, layer, (k, tn), lambda i, j: (0, j))],
        out_specs=pl.BlockSpec((tm, tn), lambda i, j: (i, j)),
        out_shape=jax.ShapeDtypeStruct((m, n), BF16),
        compiler_params=_cparams("parallel", "arbitrary"),
        name="ffn_gateup",
    )(h, wg, wu)


def _mm_k_residual_kernel(a_ref, w_ref, x_ref, o_ref, acc_ref):
    kk = pl.program_id(2)
    part = _dot(a_ref[...], w_ref[...])

    @pl.when(kk == 0)
    def _():
        acc_ref[...] = part

    @pl.when(kk > 0)
    def _():
        acc_ref[...] += part

    @pl.when(kk == pl.num_programs(2) - 1)
    def _():
        o_ref[...] = x_ref[...] + acc_ref[...]


def matmul_k_residual(a, w, x, *, tk, tn=1024, name, layer=0):
    m, k = a.shape
    n = w.shape[-1]
    tm = min(TM, m)
    assert w.shape[-2] == k and k % tk == 0 and n % tn == 0
    return pl.pallas_call(
        _mm_k_residual_kernel,
        grid=(m // tm, n // tn, k // tk),
        in_specs=[pl.BlockSpec((tm, tk), lambda i, j, kk: (i, kk)),
                  _w_spec(w, layer, (tk, tn), lambda i, j, kk: (kk, j)),
                  pl.BlockSpec((tm, tn), lambda i, j, kk: (i, j))],
        out_specs=pl.BlockSpec((tm, tn), lambda i, j, kk: (i, j)),
        out_shape=jax.ShapeDtypeStruct((m, n), F32),
        scratch_shapes=[pltpu.VMEM((tm, tn), F32)],
        compiler_params=_cparams("parallel", "parallel", "arbitrary"),
        name=name,
    )(a, w, x)


def _cumsum_kernel(x_ref, o_ref, carry_ref):
    @pl.when(pl.program_id(0) == 0)
    def _():
        carry_ref[...] = jnp.zeros_like(carry_ref)

    n = x_ref.shape[0]
    tri = _lower_tri(n).astype(F32)
    c = _dot_f32(tri, x_ref[...]) + carry_ref[0:1, :]
    o_ref[...] = c
    carry_ref[...] = jnp.broadcast_to(c[n - 1:n, :], carry_ref.shape)


def cumsum_rows(x):
    m, n = x.shape
    tb = min(CUMSUM_BLOCK, m)
    return pl.pallas_call(
        _cumsum_kernel,
        grid=(m // tb,),
        in_specs=[pl.BlockSpec((tb, n), lambda i: (i, 0))],
        out_specs=pl.BlockSpec((tb, n), lambda i: (i, 0)),
        out_shape=jax.ShapeDtypeStruct((m, n), F32),
        scratch_shapes=[pltpu.VMEM((SUBLANES, n), F32)],
        compiler_params=_cparams("arbitrary"),
        name="cumsum_rows",
    )(x)


def _retention_kernel(lg_ref, q_ref, k_ref, v_ref, g_ref, o_ref, state_ref, *, chunk, n_sub):
    head = pl.program_id(0)

    @pl.when(pl.program_id(1) == 0)
    def _():
        state_ref[...] = jnp.zeros_like(state_ref)

    lg = lg_ref[head]
    L = chunk
    ti = lax.broadcasted_iota(jnp.int32, (L, L), 0)
    si = lax.broadcasted_iota(jnp.int32, (L, L), 1)
    causal = ti >= si
    diff = jnp.where(causal, (ti - si).astype(F32), 0.0)
    decay_in = jnp.where(causal, jnp.exp(lg * diff), 0.0)
    idx = lax.broadcasted_iota(jnp.int32, (L, 1), 0).astype(F32)
    q_decay = jnp.exp(lg * (idx + 1.0))
    k_decay = jnp.exp(lg * (L - 1.0 - idx))
    chunk_decay = jnp.exp(lg * float(L))

    for u in range(n_sub):
        rows = slice(u * L, (u + 1) * L)
        q = q_ref[rows, :]
        k = k_ref[rows, :]
        v = v_ref[rows, :]
        state = state_ref[...]
        scores = _dot_nt(q, k) * decay_in
        inner = _dot(scores.astype(BF16), v)
        cross = _dot(q, state.astype(BF16)) * q_decay
        out = inner + cross
        kd = (k.astype(F32) * k_decay).astype(BF16)
        state_ref[...] = state * chunk_decay + _dot_tn(kd, v)
        ms = jnp.mean(out * out, axis=-1, keepdims=True)
        o_ref[rows, :] = (out * lax.rsqrt(ms + EPS) * g_ref[rows, :].astype(F32)).astype(o_ref.dtype)


def retention(proj, log_gamma):
    s = proj.shape[0]
    chunk = min(RET_CHUNK, s)
    tb = min(RET_BLOCK, s)
    return pl.pallas_call(
        functools.partial(_retention_kernel, chunk=chunk, n_sub=tb // chunk),
        grid=(RET_HEADS, s // tb),
        in_specs=[pl.BlockSpec(memory_space=pltpu.SMEM),
                  pl.BlockSpec((tb, RET_DIM), lambda h, c: (c, h)),
                  pl.BlockSpec((tb, RET_DIM), lambda h, c: (c, RET_HEADS + h)),
                  pl.BlockSpec((tb, RET_DIM), lambda h, c: (c, 2 * RET_HEADS + h)),
                  pl.BlockSpec((tb, RET_DIM), lambda h, c: (c, 3 * RET_HEADS + h))],
        out_specs=pl.BlockSpec((tb, RET_DIM), lambda h, c: (c, h)),
        out_shape=jax.ShapeDtypeStruct((s, RET_WIDTH), BF16),
        scratch_shapes=[pltpu.VMEM((RET_DIM, RET_DIM), F32)],
        compiler_params=_cparams("parallel", "arbitrary"),
        name="retention",
    )(log_gamma, proj, proj, proj, proj)


def _split3(x):
    hi = x.astype(BF16)
    r1 = x - hi.astype(F32)
    mid = r1.astype(BF16)
    lo = (r1 - mid.astype(F32)).astype(BF16)
    return hi, mid, lo


def _fox_prep_kernel(q_ref, k_ref, v_ref, c_ref, qa_ref, ka_ref, va_ref):
    tb = q_ref.shape[0]
    d = FOX_HEAD_DIM
    lane = lax.broadcasted_iota(jnp.int32, (tb, d), 1)
    one = jnp.ones((tb, d), F32)
    zero = jnp.zeros((tb, d), F32)
    c_all = c_ref[...] * LOG2E
    for hd in range(FOX_HEADS):
        hi, mid, lo = [jnp.broadcast_to(part.astype(F32), (tb, d)) for part in _split3(c_all[:, hd:hd + 1])]
        aug_q = jnp.where(lane == 0, hi, jnp.where(lane == 1, mid, jnp.where(lane == 2, lo,
                          jnp.where(lane < 6, one, zero))))
        aug_k = jnp.where(lane < 3, one, jnp.where(lane == 3, -hi, jnp.where(lane == 4, -mid,
                          jnp.where(lane == 5, -lo, zero))))
        src = slice(hd * d, (hd + 1) * d)
        data = slice(2 * hd * d, (2 * hd + 1) * d)
        aug = slice((2 * hd + 1) * d, (2 * hd + 2) * d)
        qa_ref[:, data] = q_ref[:, src]
        qa_ref[:, aug] = aug_q.astype(BF16)
        ka_ref[:, data] = k_ref[:, src]
        ka_ref[:, aug] = aug_k.astype(BF16)
        va_ref[:, data] = v_ref[:, src]
        va_ref[:, aug] = one.astype(BF16)


def fox_prep(proj, c, tb=512):
    s = proj.shape[0]
    tb = min(tb, s)
    w = FOX_WIDTH
    first = 4 * RET_WIDTH // w
    out = jax.ShapeDtypeStruct((s, 2 * w), BF16)
    return pl.pallas_call(
        _fox_prep_kernel,
        grid=(s // tb,),
        in_specs=[pl.BlockSpec((tb, w), lambda i: (i, first)),
                  pl.BlockSpec((tb, w), lambda i: (i, first + 1)),
                  pl.BlockSpec((tb, w), lambda i: (i, first + 2)),
                  pl.BlockSpec((tb, LANES), lambda i: (i, 0))],
        out_specs=[pl.BlockSpec((tb, 2 * w), lambda i: (i, 0))] * 3,
        out_shape=[out, out, out],
        compiler_params=_cparams("parallel"),
        name="fox_prep",
    )(proj, proj, proj, c)


def _fox_kernel(q_ref, k_ref, v_ref, o_ref, s0_ref, s1_ref, m_ref, acc_ref, *, blk, sub):
    qi = pl.program_id(1)
    d = FOX_HEAD_DIM
    q = q_ref[...]

    def scores(jsub):
        start = pl.multiple_of(jsub * sub, sub)
        return _dot_nt(q, k_ref[pl.ds(start, sub), :])

    def process(s_ref, jsub, mask_off):
        start = pl.multiple_of(jsub * sub, sub)
        s = s_ref[...]
        if mask_off is not None:
            r = lax.broadcasted_iota(jnp.int32, (blk, sub), 0)
            c = lax.broadcasted_iota(jnp.int32, (blk, sub), 1)
            s = jnp.where(r >= c + mask_off, s, NEG_BIG)
        m_old = m_ref[...]
        m_new = jnp.maximum(m_old, jnp.max(s, axis=1, keepdims=True))
        alpha = jnp.exp2(m_old - m_new)
        p = jnp.concatenate([jnp.exp2(s[:, u * d:(u + 1) * d] - m_new) for u in range(sub // d)], axis=1)
        pv = _dot(p.astype(BF16), v_ref[pl.ds(start, sub), :])
        acc_ref[:, :d] = alpha * acc_ref[:, :d] + pv[:, :d]
        acc_ref[:, d:] = alpha * acc_ref[:, d:] + pv[:, d:]
        m_ref[...] = m_new

    m_ref[...] = jnp.full_like(m_ref, NEG_BIG)
    acc_ref[...] = jnp.zeros_like(acc_ref)
    s0_ref[...] = scores(0)

    def body(t, carry):
        s1_ref[...] = scores(2 * t + 1)
        process(s0_ref, 2 * t, None)
        s0_ref[...] = scores(2 * t + 2)
        process(s1_ref, 2 * t + 1, None)
        return carry

    lax.fori_loop(0, qi, body, 0)
    s1_ref[...] = scores(2 * qi + 1)
    process(s0_ref, 2 * qi, 0)
    process(s1_ref, 2 * qi + 1, sub)
    o_ref[...] = (acc_ref[:, :d] / acc_ref[:, d:]).astype(o_ref.dtype)


def fox_attention(qa, ka, va):
    s = qa.shape[0]
    blk = min(FOX_BLOCK, s)
    sub = blk // 2
    d = FOX_HEAD_DIM
    return pl.pallas_call(
        functools.partial(_fox_kernel, blk=blk, sub=sub),
        grid=(FOX_HEADS, s // blk),
        in_specs=[pl.BlockSpec((blk, 2 * d), lambda h, i: (i, h)),
                  pl.BlockSpec((s, 2 * d), lambda h, i: (0, h)),
                  pl.BlockSpec((s, 2 * d), lambda h, i: (0, h))],
        out_specs=pl.BlockSpec((blk, d), lambda h, i: (i, h)),
        out_shape=jax.ShapeDtypeStruct((s, FOX_WIDTH), BF16),
        scratch_shapes=[pltpu.VMEM((blk, sub), F32), pltpu.VMEM((blk, sub), F32),
                        pltpu.VMEM((blk, d), F32), pltpu.VMEM((blk, 2 * d), F32)],
        compiler_params=_cparams("parallel", "arbitrary"),
        name="fox_attention",
    )(qa, ka, va)


def _rms_scale(x, g):
    return x * lax.rsqrt(jnp.mean(x * x, axis=-1, keepdims=True) + EPS) * g


def _xattn_kernel(x_ref, g_in_ref, g_out_ref, wq_ref, kt_ref, v_ref, wo_ref, o_ref, h_out_ref):
    x = x_ref[...]
    h = _rms_scale(x, g_in_ref[...]).astype(BF16)
    q = _dot(h, wq_ref[...]).astype(BF16)
    heads = []
    for hd in range(XA_HEADS):
        cols = slice(hd * XA_HEAD_DIM, (hd + 1) * XA_HEAD_DIM)
        logits = _dot(q[:, cols], kt_ref[hd]) * (XA_HEAD_DIM ** -0.5)
        mx = jnp.max(logits, axis=-1, keepdims=True)
        e = jnp.exp(logits - mx)
        p = e / jnp.sum(e, axis=-1, keepdims=True)
        heads.append(_dot(p.astype(BF16), v_ref[:, cols]))
    o = jnp.concatenate(heads, axis=-1).astype(BF16)
    x_new = x + _dot(o, wo_ref[...])
    o_ref[...] = x_new
    h_out_ref[...] = _rms_scale(x_new, g_out_ref[...]).astype(BF16)


def cross_attention(x, g_in, g_out, wq, k_t, v, wo, layer, tm=256):
    s, d = x.shape
    tm = min(tm, s)
    mem_len = v.shape[0]
    const2 = lambda i: (0, 0)
    row = lambda i: (i, 0)
    return pl.pallas_call(
        _xattn_kernel,
        grid=(s // tm,),
        in_specs=[pl.BlockSpec((tm, d), row),
                  pl.BlockSpec((1, d), const2),
                  pl.BlockSpec((1, d), const2),
                  _w_spec(wq, layer, (d, XA_WIDTH), const2),
                  pl.BlockSpec((XA_HEADS, XA_HEAD_DIM, mem_len), lambda i: (0, 0, 0)),
                  pl.BlockSpec((mem_len, XA_WIDTH), const2),
                  _w_spec(wo, layer, (XA_WIDTH, d), const2)],
        out_specs=[pl.BlockSpec((tm, d), row), pl.BlockSpec((tm, d), row)],
        out_shape=[jax.ShapeDtypeStruct((s, d), F32), jax.ShapeDtypeStruct((s, d), BF16)],
        compiler_params=_cparams("parallel"),
        name="cross_attention",
    )(x, g_in.astype(F32).reshape(1, d), g_out.astype(F32).reshape(1, d), wq, k_t, v, wo)


SSD_HALO = 128
SSD_REP = 3
SSD_REP_LANES = LANES
SSD_GROUPS_PER_STEP = 1


def _ssd_constants(L):
    rows = jnp.arange((SSM_CONV - 1) * L)
    src = SSD_HALO + rows % L - (rows // L + 1)
    shift = (jnp.arange(L + SSD_HALO)[None, :] == src[:, None]).astype(BF16)
    lane = jnp.arange(SSD_REP_LANES)
    part_head = jnp.where(lane < SSD_REP * SSM_HPG, lane % SSM_HPG, -1)
    e_col = (part_head[:, None] == (jnp.arange(SSM_HPG * L) // L)[None, :]).astype(BF16)
    e_head = (part_head[:, None] == (jnp.arange(SSM_GROUP_WIDTH) // SSM_HEAD_DIM)[None, :]).astype(BF16)
    return shift, e_col, e_head


def _replicate_heads(x):
    x3 = jnp.tile(x, (1,) * (x.ndim - 1) + (SSD_REP,))
    return jnp.pad(x3, [(0, 0)] * (x.ndim - 1) + [(0, SSD_REP_LANES - SSD_REP * SSM_HPG)])


def _pack_split3(x3):
    lane = lax.broadcasted_iota(jnp.int32, x3.shape, 1)
    hi = x3.astype(BF16).astype(F32)
    r1 = x3 - hi
    mid = r1.astype(BF16).astype(F32)
    lo = r1 - mid
    parts = jnp.where(lane < SSM_HPG, hi, jnp.where(lane < 2 * SSM_HPG, mid, lo))
    return jnp.where(lane < SSD_REP * SSM_HPG, parts, 0.0).astype(BF16)


def _ssd_kernel(x_ref, b_ref, c_ref, wx_ref, wb_ref, wc_ref, bx_ref, bb_ref, bc_ref,
                shift_ref, ecol_ref, ehead_ref, dt_ref, dtt_ref, a_ref, at_ref, dskip_ref, z_ref, gnw_ref,
                y_ref, state_ref, *halo_refs, L):
    gw = SSM_GROUP_WIDTH
    n = SSM_STATE

    @pl.when(pl.program_id(1) == 0)
    def _():
        state_ref[...] = jnp.zeros_like(state_ref)
        for halo_ref in halo_refs:
            halo_ref[...] = jnp.zeros_like(halo_ref)

    causal = _lower_tri(L)
    tri = causal.astype(F32)
    first_half = lax.broadcasted_iota(jnp.int32, (L, LANES), 1) < SSM_HEAD_DIM

    for u in range(SSD_GROUPS_PER_STEP):
        wide_cols = slice(u * gw, (u + 1) * gw)
        state_cols = slice(u * n, (u + 1) * n)

        cur16 = jnp.concatenate([x_ref[:, wide_cols], b_ref[:, state_cols], c_ref[:, state_cols]], axis=1)
        buf = jnp.concatenate([halo_refs[u][...], cur16], axis=0)
        shifted = _dot(shift_ref[...], buf)
        cur = cur16.astype(F32)
        halo_refs[u][...] = cur16[L - SSD_HALO:, :]

        def conv_silu(cols, w, bias):
            y = bias + w[SSM_CONV - 1:SSM_CONV, :] * cur[:, cols]
            for back in range(1, SSM_CONV):
                y = y + w[SSM_CONV - 1 - back:SSM_CONV - back, :] * shifted[(back - 1) * L:back * L, cols]
            return _silu(y)

        xs = conv_silu(slice(0, gw), wx_ref[:, wide_cols], bx_ref[:, wide_cols])
        bm16 = conv_silu(slice(gw, gw + n), wb_ref[:, state_cols], bb_ref[:, state_cols]).astype(BF16)
        cm16 = conv_silu(slice(gw + n, gw + 2 * n), wc_ref[:, state_cols], bc_ref[:, state_cols]).astype(BF16)

        dt3 = dt_ref[u]
        acs3 = _dot_f32(tri, dt3 * a_ref[u])
        acs_t = lax.dot_general(dtt_ref[u] * at_ref[u], tri, (((1,), (1,)), ((), ())),
                                preferred_element_type=F32, precision=lax.Precision.HIGHEST)
        acs_last = acs3[L - 1:L, :]
        lhs = jnp.concatenate([_pack_split3(dt3), _pack_split3(jnp.exp2(acs3)),
                               _pack_split3(dt3 * jnp.exp2(acs_last - acs3))], axis=0)
        wide = _dot(lhs, ehead_ref[...])
        dt_w = wide[0:L, :]
        exp_acs_w = wide[L:2 * L, :]
        dt_tail_w = wide[2 * L:, :]
        acs_col = _dot(_pack_split3(acs3), ecol_ref[...])

        cb = _dot_nt(cm16, bm16)
        state = state_ref[u]
        y_off = _dot(cm16, state.astype(BF16))

        xdt16 = (xs * dt_w).astype(BF16)
        y_parts = []
        for pr in range(SSM_HPG // 2):
            e0, e1 = 2 * pr, 2 * pr + 1
            m0 = jnp.exp2(jnp.where(causal, acs_col[:, e0 * L:(e0 + 1) * L] - acs_t[e0:e0 + 1, :], NEG_BIG)) * cb
            m1 = jnp.exp2(jnp.where(causal, acs_col[:, e1 * L:(e1 + 1) * L] - acs_t[e1:e1 + 1, :], NEG_BIG)) * cb
            mcat = jnp.concatenate([m0, m1], axis=1).astype(BF16)
            pair16 = xdt16[:, pr * LANES:(pr + 1) * LANES]
            zero = jnp.zeros_like(pair16)
            bd = jnp.concatenate([jnp.where(first_half, pair16, zero),
                                  jnp.where(first_half, zero, pair16)], axis=0)
            y_parts.append(_dot(mcat, bd))
        y = jnp.concatenate(y_parts, axis=1) + y_off * exp_acs_w
        state_ref[u] = state * exp_acs_w[L - 1:L, :] + _dot_tn(bm16, (xs * dt_tail_w).astype(BF16))

        y = (y + xs * dskip_ref[:, wide_cols]) * z_ref[:, wide_cols].astype(F32)
        ms = jnp.mean(y * y, axis=-1, keepdims=True)
        y_ref[:, wide_cols] = (y * lax.rsqrt(ms + EPS) * gnw_ref[:, wide_cols]).astype(y_ref.dtype)


def ssd_mixer(proj, conv_w, conv_b, dt_g, dt_gt, a_g, a_gt, dskip_row, gn_w):
    s = proj.shape[0]
    L = min(SSD_CHUNK, s)
    p = SSD_GROUPS_PER_STEP
    gw = p * SSM_GROUP_WIDTH
    n = p * SSM_STATE
    xb = SSM_INNER // n
    cbk = xb + SSM_GROUPS // p
    pz = SSM_INNER // gw
    pb = SSM_INNER // n
    rep = SSD_REP_LANES
    shift, e_col, e_head = _ssd_constants(L)
    return pl.pallas_call(
        functools.partial(_ssd_kernel, L=L),
        grid=(SSM_GROUPS // p, s // L),
        in_specs=[pl.BlockSpec((L, gw), lambda g, c: (c, pz + g)),
                  pl.BlockSpec((L, n), lambda g, c: (c, pb + xb + g)),
                  pl.BlockSpec((L, n), lambda g, c: (c, pb + cbk + g)),
                  pl.BlockSpec((SSM_CONV, gw), lambda g, c: (0, g)),
                  pl.BlockSpec((SSM_CONV, n), lambda g, c: (0, xb + g)),
                  pl.BlockSpec((SSM_CONV, n), lambda g, c: (0, cbk + g)),
                  pl.BlockSpec((1, gw), lambda g, c: (0, g)),
                  pl.BlockSpec((1, n), lambda g, c: (0, xb + g)),
                  pl.BlockSpec((1, n), lambda g, c: (0, cbk + g)),
                  pl.BlockSpec(shift.shape, lambda g, c: (0, 0)),
                  pl.BlockSpec(e_col.shape, lambda g, c: (0, 0)),
                  pl.BlockSpec(e_head.shape, lambda g, c: (0, 0)),
                  pl.BlockSpec((p, L, rep), lambda g, c: (g, c, 0)),
                  pl.BlockSpec((p, SSM_HPG, L), lambda g, c: (g, 0, c)),
                  pl.BlockSpec((p, 1, rep), lambda g, c: (g, 0, 0)),
                  pl.BlockSpec((p, SSM_HPG, 1), lambda g, c: (g, 0, 0)),
                  pl.BlockSpec((1, gw), lambda g, c: (0, g)),
                  pl.BlockSpec((L, gw), lambda g, c: (c, g)),
                  pl.BlockSpec((1, gw), lambda g, c: (0, g))],
        out_specs=pl.BlockSpec((L, gw), lambda g, c: (c, g)),
        out_shape=jax.ShapeDtypeStruct((s, SSM_INNER), BF16),
        scratch_shapes=[pltpu.VMEM((p, SSM_STATE, SSM_GROUP_WIDTH), F32)]
        + [pltpu.VMEM((SSD_HALO, SSM_GROUP_WIDTH + 2 * SSM_STATE), BF16)] * p,
        compiler_params=_cparams("parallel", "arbitrary"),
        name="ssd_mixer",
    )(proj, proj, proj, conv_w, conv_w, conv_w, conv_b, conv_b, conv_b, shift, e_col, e_head,
      _replicate_heads(dt_g), dt_gt, _replicate_heads(a_g), a_gt, dskip_row, proj, gn_w)


PROJ_TN = 512
PROJ_TM = 2048
EV_PROJ_COLS = 4 * RET_WIDTH + 3 * FOX_WIDTH
OD_PROJ_COLS = SSM_INNER + SSM_CONV_DIM


def _retention_fox_layer(x, norm_g, w_in16, i, b_f, w_out16):
    s = x.shape[0]
    tm = min(TM, s)
    h = rmsnorm(x, norm_g, BF16)

    half = RET_DIM // 2
    inv = ROPE_BASE ** (-jnp.arange(half, dtype=F32) / half)
    ang = jnp.arange(s).astype(F32)[:, None] * inv[None, :]
    cos, sin = jnp.cos(ang), jnp.sin(ang)

    plain = functools.partial(_epi_scale, scale=1.0)
    fq_scale = functools.partial(_epi_scale, scale=LOG2E * FOX_HEAD_DIM ** -0.5)
    bounds = [0, 2 * RET_WIDTH, 3 * RET_WIDTH, 4 * RET_WIDTH, 4 * RET_WIDTH + FOX_WIDTH, EV_PROJ_COLS]
    epis = [_epi_rotary, plain, _epi_silu, fq_scale, plain]
    segments = [(lo // PROJ_TN, hi // PROJ_TN, e) for lo, hi, e in zip(bounds[:-1], bounds[1:], epis)]
    tmp = min(PROJ_TM, s)
    (proj,) = matmul([h], w_in16, layer=i, col_off=0, n_cols=EV_PROJ_COLS, tn=PROJ_TN, tm=tmp,
                     epilogue=functools.partial(_epi_segments, segments=segments),
                     extras=[(cos, (tmp, half), lambda i_, j: (i_, 0)), (sin, (tmp, half), lambda i_, j: (i_, 0))],
                     outs=_simple_out(s, EV_PROJ_COLS, tmp, PROJ_TN, BF16), name="ev_proj")

    w_ff = jnp.pad(w_in16[i, :, EV_PROJ_COLS:EV_PROJ_COLS + FOX_HEADS], ((0, 0), (0, LANES - FOX_HEADS)))
    b_pad = jnp.pad(b_f.astype(F32), (0, LANES - FOX_HEADS)).reshape(1, LANES)
    (log_f,) = matmul([h], w_ff, col_off=0, n_cols=LANES, tn=LANES, epilogue=_epi_log_sigmoid,
                      extras=[(b_pad, (1, LANES), lambda i_, j: (0, 0))],
                      outs=_simple_out(s, LANES, tm, LANES, F32), name="ev_proj_forget")
    c = cumsum_rows(log_f)

    log_gamma = jnp.log1p(-jnp.exp2(-5.0 - jnp.arange(RET_HEADS, dtype=F32)))
    ret = retention(proj, log_gamma)
    fox = fox_attention(*fox_prep(proj, c))

    (x_new,) = matmul([ret, fox], w_out16, layer=i, col_off=0, n_cols=D_MODEL, tn=1024, epilogue=_epi_residual,
                      extras=[(x, (tm, 1024), lambda i_, j: (i_, j))],
                      outs=_simple_out(s, D_MODEL, tm, 1024, F32), name="ev_out_proj")
    return x_new


def _mamba2_layer(x, norm_g, w_in16, i, conv_w, conv_b, dt_bias, a_log, d_skip, gn_w, w_out16):
    s = x.shape[0]
    tm = min(TM, s)
    h = rmsnorm(x, norm_g, BF16)
    tmp = min(PROJ_TM, s)
    (proj,) = matmul([h], w_in16, layer=i, col_off=0, n_cols=OD_PROJ_COLS, tn=PROJ_TN, tm=tmp, epilogue=_epi_od_proj,
                     outs=_simple_out(s, OD_PROJ_COLS, tmp, PROJ_TN, BF16), name="od_proj")
    (dt,) = matmul([h], w_in16, layer=i, col_off=OD_PROJ_COLS, n_cols=SSM_HEADS, tn=SSM_HEADS,
                   epilogue=_epi_softplus,
                   extras=[(dt_bias.astype(F32).reshape(1, SSM_HEADS), (1, SSM_HEADS), lambda i_, j: (0, 0))],
                   outs=_simple_out(s, SSM_HEADS, tm, SSM_HEADS, F32), name="od_proj_dt")

    dt_g = dt.reshape(s, SSM_GROUPS, SSM_HPG).transpose(1, 0, 2)
    dt_gt = dt_g.transpose(0, 2, 1)
    a = -jnp.exp(a_log.astype(F32)) * LOG2E
    a_g = a.reshape(SSM_GROUPS, 1, SSM_HPG)
    a_gt = a.reshape(SSM_GROUPS, SSM_HPG, 1)
    dskip_row = jnp.repeat(d_skip.astype(F32), SSM_HEAD_DIM).reshape(1, SSM_INNER)
    y = ssd_mixer(proj, conv_w.astype(F32), conv_b.astype(F32).reshape(1, SSM_CONV_DIM),
                  dt_g, dt_gt, a_g, a_gt, dskip_row, gn_w.astype(F32).reshape(1, SSM_INNER))
    return matmul_k_residual(y, w_out16, x, layer=i, tk=SSM_INNER // 4, name="od_out_proj")


def _cross_attention_layer(x, g_in, g_out, mem_n, wq16, wk16, wv16, wo16, layer):
    m = mem_n.shape[0]
    plain = functools.partial(_epi_scale, scale=1.0)
    (k,) = matmul([mem_n], wk16, layer=layer, col_off=0, n_cols=XA_WIDTH, tn=XA_WIDTH, epilogue=plain,
                  outs=_simple_out(m, XA_WIDTH, m, XA_WIDTH, BF16), name="xa_proj_k")
    (v,) = matmul([mem_n], wv16, layer=layer, col_off=0, n_cols=XA_WIDTH, tn=XA_WIDTH, epilogue=plain,
                  outs=_simple_out(m, XA_WIDTH, m, XA_WIDTH, BF16), name="xa_proj_v")
    k_t = k.reshape(m, XA_HEADS, XA_HEAD_DIM).transpose(1, 2, 0)
    return cross_attention(x, g_in, g_out, wq16, k_t, v, wo16, layer)


def kernel(x, mem, mem_norm, ev_mix_norm, ev_w_in, ev_b_f, ev_w_out, od_mix_norm, od_w_in, od_conv_w, od_conv_b,
           od_dt_bias, od_a_log, od_d_skip, od_gn_w, od_w_out, xa_norm, xa_wq, xa_wk, xa_wv, xa_wo, ffn_norm,
           ffn_w_gate, ffn_w_up, ffn_w_down, final_norm):
    b, s, d = x.shape
    assert b == 1 and d == D_MODEL
    xs = x.reshape(s, d).astype(F32)
    mem_n = rmsnorm(mem.reshape(mem.shape[1], d).astype(F32), mem_norm, BF16)

    ev_w_in16, ev_w_out16 = ev_w_in.astype(BF16), ev_w_out.astype(BF16)
    od_w_in16, od_w_out16 = od_w_in.astype(BF16), od_w_out.astype(BF16)
    xa_wq16, xa_wk16, xa_wv16, xa_wo16 = [w.astype(BF16) for w in (xa_wq, xa_wk, xa_wv, xa_wo)]
    wg16 = cast_pad_bf16(ffn_w_gate, cols_out=D_FF_PAD)
    wu16 = cast_pad_bf16(ffn_w_up, cols_out=D_FF_PAD)
    wd16 = cast_pad_bf16(ffn_w_down, rows_out=D_FF_PAD)

    depth = xa_norm.shape[0]
    for layer in range(depth):
        i = layer // 2
        if layer % 2 == 0:
            xs = _retention_fox_layer(xs, ev_mix_norm[i], ev_w_in16, i, ev_b_f[i], ev_w_out16)
        else:
            xs = _mamba2_layer(xs, od_mix_norm[i], od_w_in16, i, od_conv_w[i], od_conv_b[i], od_dt_bias[i],
                               od_a_log[i], od_d_skip[i], od_gn_w[i], od_w_out16)
        xs, h_ffn = _cross_attention_layer(xs, xa_norm[layer], ffn_norm[layer], mem_n,
                                           xa_wq16, xa_wk16, xa_wv16, xa_wo16, layer)
        a = gateup(h_ffn, wg16, wu16, layer)
        xs = matmul_k_residual(a, wd16, xs, layer=layer, tk=DOWN_TK, name="ffn_down")
    out = rmsnorm(xs, final_norm, F32)
    return out.reshape(b, s, d)
```

```python
import functools
import math

import jax
import jax.numpy as jnp
from jax import lax
from jax.experimental import pallas as pl
from jax.experimental.pallas import tpu as pltpu

F32 = jnp.float32
BF16 = jnp.bfloat16

D_MODEL = 4096
EPS = 1e-6
RET_HEADS = 8
RET_DIM = 256
RET_WIDTH = RET_HEADS * RET_DIM
ROPE_BASE = 10000.0
FOX_HEAD_DIM = 128
FOX_WIDTH = 2048
FOX_HEADS = FOX_WIDTH // FOX_HEAD_DIM
SSM_INNER = 8192
SSM_HEAD_DIM = 64
SSM_HEADS = SSM_INNER // SSM_HEAD_DIM
SSM_STATE = 128
SSM_GROUPS = 8
SSM_HPG = SSM_HEADS // SSM_GROUPS
SSM_GROUP_WIDTH = SSM_INNER // SSM_GROUPS
SSM_CONV = 4
SSM_CONV_DIM = SSM_INNER + 2 * SSM_GROUPS * SSM_STATE
XA_HEADS = 4
XA_HEAD_DIM = 128
XA_WIDTH = XA_HEADS * XA_HEAD_DIM
D_FF = 11008

V7X_VMEM_BYTES = 64 * 1024 * 1024
VMEM_LIMIT_BYTES = V7X_VMEM_BYTES - 8 * 1024 * 1024
LANES = 128
SUBLANES = 8

TM = 1024
D_FF_PAD = 11264
DOWN_TK = 2816
RET_CHUNK = 256
RET_BLOCK = 512
FOX_BLOCK = 1024
SSD_CHUNK = 128
CUMSUM_BLOCK = 256
NEG_BIG = -1e30
LOG2E = 1.4426950408889634


def _cparams(*sem):
    return pltpu.CompilerParams(dimension_semantics=sem, vmem_limit_bytes=VMEM_LIMIT_BYTES)


def _silu(x):
    h = 0.5 * x
    return h + h * jnp.tanh(h)


def _softplus(x):
    return jnp.maximum(x, 0.0) + jnp.log1p(jnp.exp(-jnp.abs(x)))


def _dot(a, b):
    return jnp.dot(a, b, preferred_element_type=F32)


def _dot_nt(a, b):
    return lax.dot_general(a, b, (((1,), (1,)), ((), ())), preferred_element_type=F32)


def _dot_tn(a, b):
    return lax.dot_general(a, b, (((0,), (0,)), ((), ())), preferred_element_type=F32)


def _dot_f32(a, b):
    return jnp.dot(a, b, preferred_element_type=F32, precision=lax.Precision.HIGHEST)


def _lower_tri(n):
    r = lax.broadcasted_iota(jnp.int32, (n, n), 0)
    c = lax.broadcasted_iota(jnp.int32, (n, n), 1)
    return r >= c


def _rmsnorm_kernel(x_ref, g_ref, o_ref):
    x = x_ref[...]
    ms = jnp.mean(x * x, axis=-1, keepdims=True)
    o_ref[...] = (x * lax.rsqrt(ms + EPS) * g_ref[...]).astype(o_ref.dtype)


def rmsnorm(x, g, out_dtype, tm=512):
    m, d = x.shape
    tm = min(tm, m)
    return pl.pallas_call(
        _rmsnorm_kernel,
        grid=(m // tm,),
        in_specs=[pl.BlockSpec((tm, d), lambda i: (i, 0)),
                  pl.BlockSpec((1, d), lambda i: (0, 0))],
        out_specs=pl.BlockSpec((tm, d), lambda i: (i, 0)),
        out_shape=jax.ShapeDtypeStruct((m, d), out_dtype),
        compiler_params=_cparams("parallel"),
        name="rmsnorm",
    )(x, g.reshape(1, d).astype(F32))


def _cast_pad_kernel(x_ref, o_ref, *, n_row_blocks_in, cols_in):
    r = pl.program_id(1)
    rows, cols_out = o_ref.shape
    vals = jnp.where(r < n_row_blocks_in, x_ref[...], 0.0).astype(o_ref.dtype)
    o_ref[:, :cols_in] = vals
    if cols_out > cols_in:
        o_ref[:, cols_in:] = jnp.zeros((rows, cols_out - cols_in), o_ref.dtype)


def cast_pad_bf16(w, rows_out=None, cols_out=None, tr=256):
    layers, rows_in, cols_in = w.shape
    rows_out = rows_out or rows_in
    cols_out = cols_out or cols_in
    assert rows_in % tr == 0 and rows_out % tr == 0
    n_in = rows_in // tr
    return pl.pallas_call(
        functools.partial(_cast_pad_kernel, n_row_blocks_in=n_in, cols_in=cols_in),
        grid=(layers, rows_out // tr),
        in_specs=[pl.BlockSpec((None, tr, cols_in), lambda l, r: (l, jnp.minimum(r, n_in - 1), 0))],
        out_specs=pl.BlockSpec((None, tr, cols_out), lambda l, r: (l, r, 0)),
        out_shape=jax.ShapeDtypeStruct((layers, rows_out, cols_out), BF16),
        compiler_params=_cparams("parallel", "parallel"),
        name="cast_pad_bf16",
    )(w)


def _mm_kernel(*refs, n_a, n_extra, epilogue):
    a_refs = refs[:n_a]
    w_ref = refs[n_a]
    extra_refs = refs[n_a + 1:n_a + 1 + n_extra]
    out_refs = refs[n_a + 1 + n_extra:]
    acc = None
    off = 0
    for a_ref in a_refs:
        k = a_ref.shape[1]
        part = _dot(a_ref[...], w_ref[off:off + k, :])
        acc = part if acc is None else acc + part
        off += k
    epilogue(acc, extra_refs, out_refs)


def _w_spec(w, layer, block, index_map):
    if w.ndim == 2:
        return pl.BlockSpec(block, index_map)
    return pl.BlockSpec((None,) + block, lambda *idx: (layer,) + index_map(*idx))


def matmul(a_list, w, *, col_off, n_cols, tn, epilogue, extras=(), outs, tm=TM, name, layer=0):
    m = a_list[0].shape[0]
    tm = min(tm, m)
    k_total = w.shape[-2]
    assert sum(a.shape[1] for a in a_list) == k_total
    assert col_off % tn == 0 and n_cols % tn == 0 and m % tm == 0
    off_blocks = col_off // tn
    in_specs = [pl.BlockSpec((tm, a.shape[1]), lambda i, j: (i, 0)) for a in a_list]
    in_specs.append(_w_spec(w, layer, (k_total, tn), lambda i, j: (0, j + off_blocks)))
    in_specs += [pl.BlockSpec(bs, im) for (_, bs, im) in extras]
    out_specs = [pl.BlockSpec(bs, im) for (_, _, bs, im) in outs]
    out_shape = [jax.ShapeDtypeStruct(s, dt) for (s, dt, _, _) in outs]
    res = pl.pallas_call(
        functools.partial(_mm_kernel, n_a=len(a_list), n_extra=len(extras), epilogue=epilogue),
        grid=(m // tm, n_cols // tn),
        in_specs=in_specs,
        out_specs=out_specs,
        out_shape=out_shape,
        compiler_params=_cparams("parallel", "arbitrary"),
        name=name,
    )(*a_list, w, *[e[0] for e in extras])
    return res


def _epi_scale(acc, extra_refs, out_refs, *, scale):
    out_refs[0][...] = (acc * scale if scale != 1.0 else acc).astype(out_refs[0].dtype)


def _epi_silu(acc, extra_refs, out_refs):
    out_refs[0][...] = _silu(acc).astype(out_refs[0].dtype)


def _epi_residual(acc, extra_refs, out_refs):
    out_refs[0][...] = extra_refs[0][...] + acc


def _epi_rotary(acc, extra_refs, out_refs):
    cos = extra_refs[0][...]
    sin = extra_refs[1][...]
    half = RET_DIM // 2
    tn = acc.shape[1]
    is_q = pl.program_id(1) * tn < RET_WIDTH
    scale = jnp.where(is_q, 1.0, RET_DIM ** -0.5).astype(F32)
    for hd in range(tn // RET_DIM):
        lo = hd * RET_DIM
        x1 = acc[:, lo:lo + half]
        x2 = acc[:, lo + half:lo + RET_DIM]
        out_refs[0][:, lo:lo + half] = ((x1 * cos - x2 * sin) * scale).astype(BF16)
        out_refs[0][:, lo + half:lo + RET_DIM] = ((x1 * sin + x2 * cos) * scale).astype(BF16)


def _epi_segments(acc, extra_refs, out_refs, *, segments):
    j = pl.program_id(1)
    for lo, hi, epi in segments:
        pl.when((j >= lo) & (j < hi))(functools.partial(epi, acc, extra_refs, out_refs))


def _epi_od_proj(acc, extra_refs, out_refs):
    is_z = pl.program_id(1) * acc.shape[1] < SSM_INNER
    out_refs[0][...] = jnp.where(is_z, _silu(acc), acc).astype(BF16)


def _epi_log_sigmoid(acc, extra_refs, out_refs):
    z = acc + extra_refs[0][...]
    out_refs[0][...] = jnp.minimum(z, 0.0) - jnp.log1p(jnp.exp(-jnp.abs(z)))


def _epi_softplus(acc, extra_refs, out_refs):
    out_refs[0][...] = _softplus(acc + extra_refs[0][...])


def _simple_out(m, n, tm, tn, dtype):
    tm = min(tm, m)
    return [((m, n), dtype, (tm, tn), lambda i, j: (i, j))]


def _gateup_kernel(h_ref, wg_ref, wu_ref, o_ref):
    h = h_ref[...]
    g = _dot(h, wg_ref[...])
    u = _dot(h, wu_ref[...])
    o_ref[...] = (_silu(g) * u).astype(o_ref.dtype)


def gateup(h, wg, wu, layer, tn=512):
    m, k = h.shape
    n = wg.shape[-1]
    tm = min(TM, m)
    return pl.pallas_call(
        _gateup_kernel,
        grid=(m // tm, n // tn),
        in_specs=[pl.BlockSpec((tm, k), lambda i, j: (i, 0)),
                  _w_spec(wg, layer, (k, tn), lambda i, j: (0, j)),
                  _w_spec(wu, layer, (k, tn), lambda i, j: (0, j))],
        out_specs=pl.BlockSpec((tm, tn), lambda i, j: (i, j)),
        out_shape=jax.ShapeDtypeStruct((m, n), BF16),
        compiler_params=_cparams("parallel", "arbitrary"),
        name="ffn_gateup",
    )(h, wg, wu)


def _mm_k_residual_kernel(a_ref, w_ref, x_ref, o_ref, acc_ref):
    kk = pl.program_id(2)
    part = _dot(a_ref[...], w_ref[...])

    @pl.when(kk == 0)
    def _():
        acc_ref[...] = part

    @pl.when(kk > 0)
    def _():
        acc_ref[...] += part

    @pl.when(kk == pl.num_programs(2) - 1)
    def _():
        o_ref[...] = x_ref[...] + acc_ref[...]


def matmul_k_residual(a, w, x, *, tk, tn=1024, name, layer=0):
    m, k = a.shape
    n = w.shape[-1]
    tm = min(TM, m)
    assert w.shape[-2] == k and k % tk == 0 and n % tn == 0
    return pl.pallas_call(
        _mm_k_residual_kernel,
        grid=(m // tm, n // tn, k // tk),
        in_specs=[pl.BlockSpec((tm, tk), lambda i, j, kk: (i, kk)),
                  _w_spec(w, layer, (tk, tn), lambda i, j, kk: (kk, j)),
                  pl.BlockSpec((tm, tn), lambda i, j, kk: (i, j))],
        out_specs=pl.BlockSpec((tm, tn), lambda i, j, kk: (i, j)),
        out_shape=jax.ShapeDtypeStruct((m, n), F32),
        scratch_shapes=[pltpu.VMEM((tm, tn), F32)],
        compiler_params=_cparams("parallel", "parallel", "arbitrary"),
        name=name,
    )(a, w, x)


def _cumsum_kernel(x_ref, o_ref, carry_ref):
    @pl.when(pl.program_id(0) == 0)
    def _():
        carry_ref[...] = jnp.zeros_like(carry_ref)

    n = x_ref.shape[0]
    tri = _lower_tri(n).astype(F32)
    c = _dot_f32(tri, x_ref[...]) + carry_ref[0:1, :]
    o_ref[...] = c
    carry_ref[...] = jnp.broadcast_to(c[n - 1:n, :], carry_ref.shape)


def cumsum_rows(x):
    m, n = x.shape
    tb = min(CUMSUM_BLOCK, m)
    return pl.pallas_call(
        _cumsum_kernel,
        grid=(m // tb,),
        in_specs=[pl.BlockSpec((tb, n), lambda i: (i, 0))],
        out_specs=pl.BlockSpec((tb, n), lambda i: (i, 0)),
        out_shape=jax.ShapeDtypeStruct((m, n), F32),
        scratch_shapes=[pltpu.VMEM((SUBLANES, n), F32)],
        compiler_params=_cparams("arbitrary"),
        name="cumsum_rows",
    )(x)


def _retention_kernel(lg_ref, q_ref, k_ref, v_ref, g_ref, o_ref, state_ref, *, chunk, n_sub):
    head = pl.program_id(0)

    @pl.when(pl.program_id(1) == 0)
    def _():
        state_ref[...] = jnp.zeros_like(state_ref)

    lg = lg_ref[head]
    L = chunk
    ti = lax.broadcasted_iota(jnp.int32, (L, L), 0)
    si = lax.broadcasted_iota(jnp.int32, (L, L), 1)
    causal = ti >= si
    diff = jnp.where(causal, (ti - si).astype(F32), 0.0)
    decay_in = jnp.where(causal, jnp.exp(lg * diff), 0.0)
    idx = lax.broadcasted_iota(jnp.int32, (L, 1), 0).astype(F32)
    q_decay = jnp.exp(lg * (idx + 1.0))
    k_decay = jnp.exp(lg * (L - 1.0 - idx))
    chunk_decay = jnp.exp(lg * float(L))

    for u in range(n_sub):
        rows = slice(u * L, (u + 1) * L)
        q = q_ref[rows, :]
        k = k_ref[rows, :]
        v = v_ref[rows, :]
        state = state_ref[...]
        scores = _dot_nt(q, k) * decay_in
        inner = _dot(scores.astype(BF16), v)
        cross = _dot(q, state.astype(BF16)) * q_decay
        out = inner + cross
        kd = (k.astype(F32) * k_decay).astype(BF16)
        state_ref[...] = state * chunk_decay + _dot_tn(kd, v)
        ms = jnp.mean(out * out, axis=-1, keepdims=True)
        o_ref[rows, :] = (out * lax.rsqrt(ms + EPS) * g_ref[rows, :].astype(F32)).astype(o_ref.dtype)


def retention(proj, log_gamma):
    s = proj.shape[0]
    chunk = min(RET_CHUNK, s)
    tb = min(RET_BLOCK, s)
    return pl.pallas_call(
        functools.partial(_retention_kernel, chunk=chunk, n_sub=tb // chunk),
        grid=(RET_HEADS, s // tb),
        in_specs=[pl.BlockSpec(memory_space=pltpu.SMEM),
                  pl.BlockSpec((tb, RET_DIM), lambda h, c: (c, h)),
                  pl.BlockSpec((tb, RET_DIM), lambda h, c: (c, RET_HEADS + h)),
                  pl.BlockSpec((tb, RET_DIM), lambda h, c: (c, 2 * RET_HEADS + h)),
                  pl.BlockSpec((tb, RET_DIM), lambda h, c: (c, 3 * RET_HEADS + h))],
        out_specs=pl.BlockSpec((tb, RET_DIM), lambda h, c: (c, h)),
        out_shape=jax.ShapeDtypeStruct((s, RET_WIDTH), BF16),
        scratch_shapes=[pltpu.VMEM((RET_DIM, RET_DIM), F32)],
        compiler_params=_cparams("parallel", "arbitrary"),
        name="retention",
    )(log_gamma, proj, proj, proj, proj)


def _split3(x):
    hi = x.astype(BF16)
    r1 = x - hi.astype(F32)
    mid = r1.astype(BF16)
    lo = (r1 - mid.astype(F32)).astype(BF16)
    return hi, mid, lo


def _fox_prep_kernel(q_ref, k_ref, v_ref, c_ref, qa_ref, ka_ref, va_ref):
    tb = q_ref.shape[0]
    d = FOX_HEAD_DIM
    lane = lax.broadcasted_iota(jnp.int32, (tb, d), 1)
    one = jnp.ones((tb, d), F32)
    zero = jnp.zeros((tb, d), F32)
    c_all = c_ref[...] * LOG2E
    for hd in range(FOX_HEADS):
        hi, mid, lo = [jnp.broadcast_to(part.astype(F32), (tb, d)) for part in _split3(c_all[:, hd:hd + 1])]
        aug_q = jnp.where(lane == 0, hi, jnp.where(lane == 1, mid, jnp.where(lane == 2, lo,
                          jnp.where(lane < 6, one, zero))))
        aug_k = jnp.where(lane < 3, one, jnp.where(lane == 3, -hi, jnp.where(lane == 4, -mid,
                          jnp.where(lane == 5, -lo, zero))))
        src = slice(hd * d, (hd + 1) * d)
        data = slice(2 * hd * d, (2 * hd + 1) * d)
        aug = slice((2 * hd + 1) * d, (2 * hd + 2) * d)
        qa_ref[:, data] = q_ref[:, src]
        qa_ref[:, aug] = aug_q.astype(BF16)
        ka_ref[:, data] = k_ref[:, src]
        ka_ref[:, aug] = aug_k.astype(BF16)
        va_ref[:, data] = v_ref[:, src]
        va_ref[:, aug] = one.astype(BF16)


def fox_prep(proj, c, tb=512):
    s = proj.shape[0]
    tb = min(tb, s)
    w = FOX_WIDTH
    first = 4 * RET_WIDTH // w
    out = jax.ShapeDtypeStruct((s, 2 * w), BF16)
    return pl.pallas_call(
        _fox_prep_kernel,
        grid=(s // tb,),
        in_specs=[pl.BlockSpec((tb, w), lambda i: (i, first)),
                  pl.BlockSpec((tb, w), lambda i: (i, first + 1)),
                  pl.BlockSpec((tb, w), lambda i: (i, first + 2)),
                  pl.BlockSpec((tb, LANES), lambda i: (i, 0))],
        out_specs=[pl.BlockSpec((tb, 2 * w), lambda i: (i, 0))] * 3,
        out_shape=[out, out, out],
        compiler_params=_cparams("parallel"),
        name="fox_prep",
    )(proj, proj, proj, c)


def _fox_kernel(q_ref, k_ref, v_ref, o_ref, s0_ref, s1_ref, m_ref, acc_ref, *, blk, sub):
    qi = pl.program_id(1)
    d = FOX_HEAD_DIM
    q = q_ref[...]

    def scores(jsub):
        start = pl.multiple_of(jsub * sub, sub)
        return _dot_nt(q, k_ref[pl.ds(start, sub), :])

    def process(s_ref, jsub, mask_off):
        start = pl.multiple_of(jsub * sub, sub)
        s = s_ref[...]
        if mask_off is not None:
            r = lax.broadcasted_iota(jnp.int32, (blk, sub), 0)
            c = lax.broadcasted_iota(jnp.int32, (blk, sub), 1)
            s = jnp.where(r >= c + mask_off, s, NEG_BIG)
        m_old = m_ref[...]
        m_new = jnp.maximum(m_old, jnp.max(s, axis=1, keepdims=True))
        alpha = jnp.exp2(m_old - m_new)
        p = jnp.concatenate([jnp.exp2(s[:, u * d:(u + 1) * d] - m_new) for u in range(sub // d)], axis=1)
        pv = _dot(p.astype(BF16), v_ref[pl.ds(start, sub), :])
        acc_ref[:, :d] = alpha * acc_ref[:, :d] + pv[:, :d]
        acc_ref[:, d:] = alpha * acc_ref[:, d:] + pv[:, d:]
        m_ref[...] = m_new

    m_ref[...] = jnp.full_like(m_ref, NEG_BIG)
    acc_ref[...] = jnp.zeros_like(acc_ref)
    s0_ref[...] = scores(0)

    def body(t, carry):
        s1_ref[...] = scores(2 * t + 1)
        process(s0_ref, 2 * t, None)
        s0_ref[...] = scores(2 * t + 2)
        process(s1_ref, 2 * t + 1, None)
        return carry

    lax.fori_loop(0, qi, body, 0)
    s1_ref[...] = scores(2 * qi + 1)
    process(s0_ref, 2 * qi, 0)
    process(s1_ref, 2 * qi + 1, sub)
    o_ref[...] = (acc_ref[:, :d] / acc_ref[:, d:]).astype(o_ref.dtype)


def fox_attention(qa, ka, va):
    s = qa.shape[0]
    blk = min(FOX_BLOCK, s)
    sub = blk // 2
    d = FOX_HEAD_DIM
    return pl.pallas_call(
        functools.partial(_fox_kernel, blk=blk, sub=sub),
        grid=(FOX_HEADS, s // blk),
        in_specs=[pl.BlockSpec((blk, 2 * d), lambda h, i: (i, h)),
                  pl.BlockSpec((s, 2 * d), lambda h, i: (0, h)),
                  pl.BlockSpec((s, 2 * d), lambda h, i: (0, h))],
        out_specs=pl.BlockSpec((blk, d), lambda h, i: (i, h)),
        out_shape=jax.ShapeDtypeStruct((s, FOX_WIDTH), BF16),
        scratch_shapes=[pltpu.VMEM((blk, sub), F32), pltpu.VMEM((blk, sub), F32),
                        pltpu.VMEM((blk, d), F32), pltpu.VMEM((blk, 2 * d), F32)],
        compiler_params=_cparams("parallel", "arbitrary"),
        name="fox_attention",
    )(qa, ka, va)


def _rms_scale(x, g):
    return x * lax.rsqrt(jnp.mean(x * x, axis=-1, keepdims=True) + EPS) * g


def _xattn_kernel(x_ref, g_in_ref, g_out_ref, wq_ref, kt_ref, v_ref, wo_ref, o_ref, h_out_ref):
    x = x_ref[...]
    h = _rms_scale(x, g_in_ref[...]).astype(BF16)
    q = _dot(h, wq_ref[...]).astype(BF16)
    heads = []
    for hd in range(XA_HEADS):
        cols = slice(hd * XA_HEAD_DIM, (hd + 1) * XA_HEAD_DIM)
        logits = _dot(q[:, cols], kt_ref[hd]) * (XA_HEAD_DIM ** -0.5)
        mx = jnp.max(logits, axis=-1, keepdims=True)
        e = jnp.exp(logits - mx)
        p = e / jnp.sum(e, axis=-1, keepdims=True)
        heads.append(_dot(p.astype(BF16), v_ref[:, cols]))
    o = jnp.concatenate(heads, axis=-1).astype(BF16)
    x_new = x + _dot(o, wo_ref[...])
    o_ref[...] = x_new
    h_out_ref[...] = _rms_scale(x_new, g_out_ref[...]).astype(BF16)


def cross_attention(x, g_in, g_out, wq, k_t, v, wo, layer, tm=512):
    s, d = x.shape
    tm = min(tm, s)
    mem_len = v.shape[0]
    const2 = lambda i: (0, 0)
    row = lambda i: (i, 0)
    once = pl.Buffered(1)
    return pl.pallas_call(
        _xattn_kernel,
        grid=(s // tm,),
        in_specs=[pl.BlockSpec((tm, d), row),
                  pl.BlockSpec((1, d), const2),
                  pl.BlockSpec((1, d), const2),
                  pl.BlockSpec((None, d, XA_WIDTH), lambda i: (layer, 0, 0), pipeline_mode=once),
                  pl.BlockSpec((XA_HEADS, XA_HEAD_DIM, mem_len), lambda i: (0, 0, 0)),
                  pl.BlockSpec((mem_len, XA_WIDTH), const2),
                  pl.BlockSpec((None, XA_WIDTH, d), lambda i: (layer, 0, 0), pipeline_mode=once)],
        out_specs=[pl.BlockSpec((tm, d), row), pl.BlockSpec((tm, d), row)],
        out_shape=[jax.ShapeDtypeStruct((s, d), F32), jax.ShapeDtypeStruct((s, d), BF16)],
        compiler_params=_cparams("parallel"),
        name="cross_attention",
    )(x, g_in.astype(F32).reshape(1, d), g_out.astype(F32).reshape(1, d), wq, k_t, v, wo)


SSD_HALO = 128
SSD_REP = 3
SSD_REP_LANES = LANES
SSD_GROUPS_PER_STEP = 1


def _ssd_constants(L):
    rows = jnp.arange((SSM_CONV - 1) * L)
    src = SSD_HALO + rows % L - (rows // L + 1)
    shift = (jnp.arange(L + SSD_HALO)[None, :] == src[:, None]).astype(BF16)
    lane = jnp.arange(SSD_REP_LANES)
    part_head = jnp.where(lane < SSD_REP * SSM_HPG, lane % SSM_HPG, -1)
    e_col = (part_head[:, None] == (jnp.arange(SSM_HPG * L) // L)[None, :]).astype(BF16)
    e_head = (part_head[:, None] == (jnp.arange(SSM_GROUP_WIDTH) // SSM_HEAD_DIM)[None, :]).astype(BF16)
    return shift, e_col, e_head


def _replicate_heads(x):
    x3 = jnp.tile(x, (1,) * (x.ndim - 1) + (SSD_REP,))
    return jnp.pad(x3, [(0, 0)] * (x.ndim - 1) + [(0, SSD_REP_LANES - SSD_REP * SSM_HPG)])


def _pack_split3(x3):
    lane = lax.broadcasted_iota(jnp.int32, x3.shape, 1)
    hi = x3.astype(BF16).astype(F32)
    r1 = x3 - hi
    mid = r1.astype(BF16).astype(F32)
    lo = r1 - mid
    parts = jnp.where(lane < SSM_HPG, hi, jnp.where(lane < 2 * SSM_HPG, mid, lo))
    return jnp.where(lane < SSD_REP * SSM_HPG, parts, 0.0).astype(BF16)


def _ssd_kernel(x_ref, b_ref, c_ref, wx_ref, wb_ref, wc_ref, bx_ref, bb_ref, bc_ref,
                shift_ref, ecol_ref, ehead_ref, dt_ref, dtt_ref, a_ref, at_ref, dskip_ref, z_ref, gnw_ref,
                y_ref, state_ref, *halo_refs, L):
    gw = SSM_GROUP_WIDTH
    n = SSM_STATE

    @pl.when(pl.program_id(1) == 0)
    def _():
        state_ref[...] = jnp.zeros_like(state_ref)
        for halo_ref in halo_refs:
            halo_ref[...] = jnp.zeros_like(halo_ref)

    causal = _lower_tri(L)
    tri = causal.astype(F32)
    first_half = lax.broadcasted_iota(jnp.int32, (L, LANES), 1) < SSM_HEAD_DIM

    for u in range(SSD_GROUPS_PER_STEP):
        wide_cols = slice(u * gw, (u + 1) * gw)
        state_cols = slice(u * n, (u + 1) * n)

        cur16 = jnp.concatenate([x_ref[:, wide_cols], b_ref[:, state_cols], c_ref[:, state_cols]], axis=1)
        buf = jnp.concatenate([halo_refs[u][...], cur16], axis=0)
        shifted = _dot(shift_ref[...], buf)
        cur = cur16.astype(F32)
        halo_refs[u][...] = cur16[L - SSD_HALO:, :]

        def conv_silu(cols, w, bias):
            y = bias + w[SSM_CONV - 1:SSM_CONV, :] * cur[:, cols]
            for back in range(1, SSM_CONV):
                y = y + w[SSM_CONV - 1 - back:SSM_CONV - back, :] * shifted[(back - 1) * L:back * L, cols]
            return _silu(y)

        xs = conv_silu(slice(0, gw), wx_ref[:, wide_cols], bx_ref[:, wide_cols])
        bm16 = conv_silu(slice(gw, gw + n), wb_ref[:, state_cols], bb_ref[:, state_cols]).astype(BF16)
        cm16 = conv_silu(slice(gw + n, gw + 2 * n), wc_ref[:, state_cols], bc_ref[:, state_cols]).astype(BF16)

        dt3 = dt_ref[u]
        acs3 = _dot_f32(tri, dt3 * a_ref[u])
        acs_t = lax.dot_general(dtt_ref[u] * at_ref[u], tri, (((1,), (1,)), ((), ())),
                                preferred_element_type=F32, precision=lax.Precision.HIGHEST)
        acs_last = acs3[L - 1:L, :]
        lhs = jnp.concatenate([_pack_split3(dt3), _pack_split3(jnp.exp2(acs3)),
                               _pack_split3(dt3 * jnp.exp2(acs_last - acs3))], axis=0)
        wide = _dot(lhs, ehead_ref[...])
        dt_w = wide[0:L, :]
        exp_acs_w = wide[L:2 * L, :]
        dt_tail_w = wide[2 * L:, :]
        acs_col = _dot(_pack_split3(acs3), ecol_ref[...])

        cb = _dot_nt(cm16, bm16)
        state = state_ref[u]
        y_off = _dot(cm16, state.astype(BF16))

        xdt16 = (xs * dt_w).astype(BF16)
        y_parts = []
        for pr in range(SSM_HPG // 2):
            e0, e1 = 2 * pr, 2 * pr + 1
            m0 = jnp.exp2(jnp.where(causal, acs_col[:, e0 * L:(e0 + 1) * L] - acs_t[e0:e0 + 1, :], NEG_BIG)) * cb
            m1 = jnp.exp2(jnp.where(causal, acs_col[:, e1 * L:(e1 + 1) * L] - acs_t[e1:e1 + 1, :], NEG_BIG)) * cb
            mcat = jnp.concatenate([m0, m1], axis=1).astype(BF16)
            pair16 = xdt16[:, pr * LANES:(pr + 1) * LANES]
            zero = jnp.zeros_like(pair16)
            bd = jnp.concatenate([jnp.where(first_half, pair16, zero),
                                  jnp.where(first_half, zero, pair16)], axis=0)
            y_parts.append(_dot(mcat, bd))
        y = jnp.concatenate(y_parts, axis=1) + y_off * exp_acs_w
        state_ref[u] = state * exp_acs_w[L - 1:L, :] + _dot_tn(bm16, (xs * dt_tail_w).astype(BF16))

        y = (y + xs * dskip_ref[:, wide_cols]) * z_ref[:, wide_cols].astype(F32)
        ms = jnp.mean(y * y, axis=-1, keepdims=True)
        y_ref[:, wide_cols] = (y * lax.rsqrt(ms + EPS) * gnw_ref[:, wide_cols]).astype(y_ref.dtype)


def ssd_mixer(proj, conv_w, conv_b, dt_g, dt_gt, a_g, a_gt, dskip_row, gn_w):
    s = proj.shape[0]
    L = min(SSD_CHUNK, s)
    p = SSD_GROUPS_PER_STEP
    gw = p * SSM_GROUP_WIDTH
    n = p * SSM_STATE
    xb = SSM_INNER // n
    cbk = xb + SSM_GROUPS // p
    pz = SSM_INNER // gw
    pb = SSM_INNER // n
    rep = SSD_REP_LANES
    shift, e_col, e_head = _ssd_constants(L)
    return pl.pallas_call(
        functools.partial(_ssd_kernel, L=L),
        grid=(SSM_GROUPS // p, s // L),
        in_specs=[pl.BlockSpec((L, gw), lambda g, c: (c, pz + g)),
                  pl.BlockSpec((L, n), lambda g, c: (c, pb + xb + g)),
                  pl.BlockSpec((L, n), lambda g, c: (c, pb + cbk + g)),
                  pl.BlockSpec((SSM_CONV, gw), lambda g, c: (0, g)),
                  pl.BlockSpec((SSM_CONV, n), lambda g, c: (0, xb + g)),
                  pl.BlockSpec((SSM_CONV, n), lambda g, c: (0, cbk + g)),
                  pl.BlockSpec((1, gw), lambda g, c: (0, g)),
                  pl.BlockSpec((1, n), lambda g, c: (0, xb + g)),
                  pl.BlockSpec((1, n), lambda g, c: (0, cbk + g)),
                  pl.BlockSpec(shift.shape, lambda g, c: (0, 0)),
                  pl.BlockSpec(e_col.shape, lambda g, c: (0, 0)),
                  pl.BlockSpec(e_head.shape, lambda g, c: (0, 0)),
                  pl.BlockSpec((p, L, rep), lambda g, c: (g, c, 0)),
                  pl.BlockSpec((p, SSM_HPG, L), lambda g, c: (g, 0, c)),
                  pl.BlockSpec((p, 1, rep), lambda g, c: (g, 0, 0)),
                  pl.BlockSpec((p, SSM_HPG, 1), lambda g, c: (g, 0, 0)),
                  pl.BlockSpec((1, gw), lambda g, c: (0, g)),
                  pl.BlockSpec((L, gw), lambda g, c: (c, g)),
                  pl.BlockSpec((1, gw), lambda g, c: (0, g))],
        out_specs=pl.BlockSpec((L, gw), lambda g, c: (c, g)),
        out_shape=jax.ShapeDtypeStruct((s, SSM_INNER), BF16),
        scratch_shapes=[pltpu.VMEM((p, SSM_STATE, SSM_GROUP_WIDTH), F32)]
        + [pltpu.VMEM((SSD_HALO, SSM_GROUP_WIDTH + 2 * SSM_STATE), BF16)] * p,
        compiler_params=_cparams("parallel", "arbitrary"),
        name="ssd_mixer",
    )(proj, proj, proj, conv_w, conv_w, conv_w, conv_b, conv_b, conv_b, shift, e_col, e_head,
      _replicate_heads(dt_g), dt_gt, _replicate_heads(a_g), a_gt, dskip_row, proj, gn_w)


PROJ_TN = 512
PROJ_TM = 2048
EV_PROJ_COLS = 4 * RET_WIDTH + 3 * FOX_WIDTH
OD_PROJ_COLS = SSM_INNER + SSM_CONV_DIM


def _retention_fox_layer(x, norm_g, w_in16, i, b_f, w_out16):
    s = x.shape[0]
    tm = min(TM, s)
    h = rmsnorm(x, norm_g, BF16)

    half = RET_DIM // 2
    inv = ROPE_BASE ** (-jnp.arange(half, dtype=F32) / half)
    ang = jnp.arange(s).astype(F32)[:, None] * inv[None, :]
    cos, sin = jnp.cos(ang), jnp.sin(ang)

    plain = functools.partial(_epi_scale, scale=1.0)
    fq_scale = functools.partial(_epi_scale, scale=LOG2E * FOX_HEAD_DIM ** -0.5)
    bounds = [0, 2 * RET_WIDTH, 3 * RET_WIDTH, 4 * RET_WIDTH, 4 * RET_WIDTH + FOX_WIDTH, EV_PROJ_COLS]
    epis = [_epi_rotary, plain, _epi_silu, fq_scale, plain]
    segments = [(lo // PROJ_TN, hi // PROJ_TN, e) for lo, hi, e in zip(bounds[:-1], bounds[1:], epis)]
    tmp = min(PROJ_TM, s)
    (proj,) = matmul([h], w_in16, layer=i, col_off=0, n_cols=EV_PROJ_COLS, tn=PROJ_TN, tm=tmp,
                     epilogue=functools.partial(_epi_segments, segments=segments),
                     extras=[(cos, (tmp, half), lambda i_, j: (i_, 0)), (sin, (tmp, half), lambda i_, j: (i_, 0))],
                     outs=_simple_out(s, EV_PROJ_COLS, tmp, PROJ_TN, BF16), name="ev_proj")

    w_ff = jnp.pad(w_in16[i, :, EV_PROJ_COLS:EV_PROJ_COLS + FOX_HEADS], ((0, 0), (0, LANES - FOX_HEADS)))
    b_pad = jnp.pad(b_f.astype(F32), (0, LANES - FOX_HEADS)).reshape(1, LANES)
    (log_f,) = matmul([h], w_ff, col_off=0, n_cols=LANES, tn=LANES, epilogue=_epi_log_sigmoid,
                      extras=[(b_pad, (1, LANES), lambda i_, j: (0, 0))],
                      outs=_simple_out(s, LANES, tm, LANES, F32), name="ev_proj_forget")
    c = cumsum_rows(log_f)

    log_gamma = jnp.log1p(-jnp.exp2(-5.0 - jnp.arange(RET_HEADS, dtype=F32)))
    ret = retention(proj, log_gamma)
    fox = fox_attention(*fox_prep(proj, c))

    (x_new,) = matmul([ret, fox], w_out16, layer=i, col_off=0, n_cols=D_MODEL, tn=1024, epilogue=_epi_residual,
                      extras=[(x, (tm, 1024), lambda i_, j: (i_, j))],
                      outs=_simple_out(s, D_MODEL, tm, 1024, F32), name="ev_out_proj")
    return x_new


def _mamba2_layer(x, norm_g, w_in16, i, conv_w, conv_b, dt_bias, a_log, d_skip, gn_w, w_out16):
    s = x.shape[0]
    tm = min(TM, s)
    h = rmsnorm(x, norm_g, BF16)
    tmp = min(PROJ_TM, s)
    (proj,) = matmul([h], w_in16, layer=i, col_off=0, n_cols=OD_PROJ_COLS, tn=PROJ_TN, tm=tmp, epilogue=_epi_od_proj,
                     outs=_simple_out(s, OD_PROJ_COLS, tmp, PROJ_TN, BF16), name="od_proj")
    (dt,) = matmul([h], w_in16, layer=i, col_off=OD_PROJ_COLS, n_cols=SSM_HEADS, tn=SSM_HEADS,
                   epilogue=_epi_softplus,
                   extras=[(dt_bias.astype(F32).reshape(1, SSM_HEADS), (1, SSM_HEADS), lambda i_, j: (0, 0))],
                   outs=_simple_out(s, SSM_HEADS, tm, SSM_HEADS, F32), name="od_proj_dt")

    dt_g = dt.reshape(s, SSM_GROUPS, SSM_HPG).transpose(1, 0, 2)
    dt_gt = dt_g.transpose(0, 2, 1)
    a = -jnp.exp(a_log.astype(F32)) * LOG2E
    a_g = a.reshape(SSM_GROUPS, 1, SSM_HPG)
    a_gt = a.reshape(SSM_GROUPS, SSM_HPG, 1)
    dskip_row = jnp.repeat(d_skip.astype(F32), SSM_HEAD_DIM).reshape(1, SSM_INNER)
    y = ssd_mixer(proj, conv_w.astype(F32), conv_b.astype(F32).reshape(1, SSM_CONV_DIM),
                  dt_g, dt_gt, a_g, a_gt, dskip_row, gn_w.astype(F32).reshape(1, SSM_INNER))
    tn = 256
    (x_new,) = matmul([y], w_out16, layer=i, col_off=0, n_cols=D_MODEL, tn=tn, epilogue=_epi_residual,
                      extras=[(x, (tm, tn), lambda i_, j: (i_, j))],
                      outs=_simple_out(s, D_MODEL, tm, tn, F32), name="od_out_proj")
    return x_new


def _cross_attention_layer(x, g_in, g_out, mem_n, wq16, wk16, wv16, wo16, layer):
    m = mem_n.shape[0]
    plain = functools.partial(_epi_scale, scale=1.0)
    (k,) = matmul([mem_n], wk16, layer=layer, col_off=0, n_cols=XA_WIDTH, tn=XA_WIDTH, epilogue=plain,
                  outs=_simple_out(m, XA_WIDTH, m, XA_WIDTH, BF16), name="xa_proj_k")
    (v,) = matmul([mem_n], wv16, layer=layer, col_off=0, n_cols=XA_WIDTH, tn=XA_WIDTH, epilogue=plain,
                  outs=_simple_out(m, XA_WIDTH, m, XA_WIDTH, BF16), name="xa_proj_v")
    k_t = k.reshape(m, XA_HEADS, XA_HEAD_DIM).transpose(1, 2, 0)
    return cross_attention(x, g_in, g_out, wq16, k_t, v, wo16, layer)


def kernel(x, mem, mem_norm, ev_mix_norm, ev_w_in, ev_b_f, ev_w_out, od_mix_norm, od_w_in, od_conv_w, od_conv_b,
           od_dt_bias, od_a_log, od_d_skip, od_gn_w, od_w_out, xa_norm, xa_wq, xa_wk, xa_wv, xa_wo, ffn_norm,
           ffn_w_gate, ffn_w_up, ffn_w_down, final_norm):
    b, s, d = x.shape
    assert b == 1 and d == D_MODEL
    xs = x.reshape(s, d).astype(F32)
    mem_n = rmsnorm(mem.reshape(mem.shape[1], d).astype(F32), mem_norm, BF16)

    ev_w_in16, ev_w_out16 = ev_w_in.astype(BF16), ev_w_out.astype(BF16)
    od_w_in16, od_w_out16 = od_w_in.astype(BF16), od_w_out.astype(BF16)
    xa_wq16, xa_wk16, xa_wv16, xa_wo16 = [w.astype(BF16) for w in (xa_wq, xa_wk, xa_wv, xa_wo)]
    wg16 = cast_pad_bf16(ffn_w_gate, cols_out=D_FF_PAD)
    wu16 = cast_pad_bf16(ffn_w_up, cols_out=D_FF_PAD)
    wd16 = cast_pad_bf16(ffn_w_down, rows_out=D_FF_PAD)

    depth = xa_norm.shape[0]
    for layer in range(depth):
        i = layer // 2
        if layer % 2 == 0:
            xs = _retention_fox_layer(xs, ev_mix_norm[i], ev_w_in16, i, ev_b_f[i], ev_w_out16)
        else:
            xs = _mamba2_layer(xs, od_mix_norm[i], od_w_in16, i, od_conv_w[i], od_conv_b[i], od_dt_bias[i],
                               od_a_log[i], od_d_skip[i], od_gn_w[i], od_w_out16)
        xs, h_ffn = _cross_attention_layer(xs, xa_norm[layer], ffn_norm[layer], mem_n,
                                           xa_wq16, xa_wk16, xa_wv16, xa_wo16, layer)
        a = gateup(h_ffn, wg16, wu16, layer)
        xs = matmul_k_residual(a, wd16, xs, layer=layer, tk=DOWN_TK, name="ffn_down")
    out = rmsnorm(xs, final_norm, F32)
    return out.reshape(b, s, d)
```

```python
import functools
import math

import jax
import jax.numpy as jnp
from jax import lax
from jax.experimental import pallas as pl
from jax.experimental.pallas import tpu as pltpu

F32 = jnp.float32
BF16 = jnp.bfloat16

D_MODEL = 4096
EPS = 1e-6
RET_HEADS = 8
RET_DIM = 256
RET_WIDTH = RET_HEADS * RET_DIM
ROPE_BASE = 10000.0
FOX_HEAD_DIM = 128
FOX_WIDTH = 2048
FOX_HEADS = FOX_WIDTH // FOX_HEAD_DIM
SSM_INNER = 8192
SSM_HEAD_DIM = 64
SSM_HEADS = SSM_INNER // SSM_HEAD_DIM
SSM_STATE = 128
SSM_GROUPS = 8
SSM_HPG = SSM_HEADS // SSM_GROUPS
SSM_GROUP_WIDTH = SSM_INNER // SSM_GROUPS
SSM_CONV = 4
SSM_CONV_DIM = SSM_INNER + 2 * SSM_GROUPS * SSM_STATE
XA_HEADS = 4
XA_HEAD_DIM = 128
XA_WIDTH = XA_HEADS * XA_HEAD_DIM
D_FF = 11008

V7X_VMEM_BYTES = 64 * 1024 * 1024
VMEM_LIMIT_BYTES = V7X_VMEM_BYTES - 8 * 1024 * 1024
LANES = 128
SUBLANES = 8

TM = 1024
D_FF_PAD = 11264
DOWN_TK = 2816
RET_CHUNK = 256
RET_BLOCK = 512
FOX_BLOCK = 1024
SSD_CHUNK = 128
CUMSUM_BLOCK = 256
NEG_BIG = -1e30
LOG2E = 1.4426950408889634


def _cparams(*sem):
    return pltpu.CompilerParams(dimension_semantics=sem, vmem_limit_bytes=VMEM_LIMIT_BYTES)


def _silu(x):
    h = 0.5 * x
    return h + h * jnp.tanh(h)


def _softplus(x):
    return jnp.maximum(x, 0.0) + jnp.log1p(jnp.exp(-jnp.abs(x)))


def _dot(a, b):
    return jnp.dot(a, b, preferred_element_type=F32)


def _dot_nt(a, b):
    return lax.dot_general(a, b, (((1,), (1,)), ((), ())), preferred_element_type=F32)


def _dot_tn(a, b):
    return lax.dot_general(a, b, (((0,), (0,)), ((), ())), preferred_element_type=F32)


def _dot_f32(a, b):
    return jnp.dot(a, b, preferred_element_type=F32, precision=lax.Precision.HIGHEST)


def _lower_tri(n):
    r = lax.broadcasted_iota(jnp.int32, (n, n), 0)
    c = lax.broadcasted_iota(jnp.int32, (n, n), 1)
    return r >= c


def _rmsnorm_kernel(x_ref, g_ref, o_ref):
    x = x_ref[...]
    ms = jnp.mean(x * x, axis=-1, keepdims=True)
    o_ref[...] = (x * lax.rsqrt(ms + EPS) * g_ref[...]).astype(o_ref.dtype)


def rmsnorm(x, g, out_dtype, tm=512):
    m, d = x.shape
    tm = min(tm, m)
    return pl.pallas_call(
        _rmsnorm_kernel,
        grid=(m // tm,),
        in_specs=[pl.BlockSpec((tm, d), lambda i: (i, 0)),
                  pl.BlockSpec((1, d), lambda i: (0, 0))],
        out_specs=pl.BlockSpec((tm, d), lambda i: (i, 0)),
        out_shape=jax.ShapeDtypeStruct((m, d), out_dtype),
        compiler_params=_cparams("parallel"),
        name="rmsnorm",
    )(x, g.reshape(1, d).astype(F32))


def _cast_pad_kernel(x_ref, o_ref, *, n_row_blocks_in, cols_in):
    r = pl.program_id(1)
    rows, cols_out = o_ref.shape
    vals = jnp.where(r < n_row_blocks_in, x_ref[...], 0.0).astype(o_ref.dtype)
    o_ref[:, :cols_in] = vals
    if cols_out > cols_in:
        o_ref[:, cols_in:] = jnp.zeros((rows, cols_out - cols_in), o_ref.dtype)


def cast_pad_bf16(w, rows_out=None, cols_out=None, tr=256):
    layers, rows_in, cols_in = w.shape
    rows_out = rows_out or rows_in
    cols_out = cols_out or cols_in
    assert rows_in % tr == 0 and rows_out % tr == 0
    n_in = rows_in // tr
    return pl.pallas_call(
        functools.partial(_cast_pad_kernel, n_row_blocks_in=n_in, cols_in=cols_in),
        grid=(layers, rows_out // tr),
        in_specs=[pl.BlockSpec((None, tr, cols_in), lambda l, r: (l, jnp.minimum(r, n_in - 1), 0))],
        out_specs=pl.BlockSpec((None, tr, cols_out), lambda l, r: (l, r, 0)),
        out_shape=jax.ShapeDtypeStruct((layers, rows_out, cols_out), BF16),
        compiler_params=_cparams("parallel", "parallel"),
        name="cast_pad_bf16",
    )(w)


def _mm_kernel(*refs, n_a, n_extra, epilogue):
    a_refs = refs[:n_a]
    w_ref = refs[n_a]
    extra_refs = refs[n_a + 1:n_a + 1 + n_extra]
    out_refs = refs[n_a + 1 + n_extra:]
    acc = None
    off = 0
    for a_ref in a_refs:
        k = a_ref.shape[1]
        part = _dot(a_ref[...], w_ref[off:off + k, :])
        acc = part if acc is None else acc + part
        off += k
    epilogue(acc, extra_refs, out_refs)


def _w_spec(w, layer, block, index_map):
    if w.ndim == 2:
        return pl.BlockSpec(block, index_map)
    return pl.BlockSpec((None,) + block, lambda *idx: (layer,) + index_map(*idx))


def matmul(a_list, w, *, col_off, n_cols, tn, epilogue, extras=(), outs, tm=TM, name, layer=0):
    m = a_list[0].shape[0]
    tm = min(tm, m)
    k_total = w.shape[-2]
    assert sum(a.shape[1] for a in a_list) == k_total
    assert col_off % tn == 0 and n_cols % tn == 0 and m % tm == 0
    off_blocks = col_off // tn
    in_specs = [pl.BlockSpec((tm, a.shape[1]), lambda i, j: (i, 0)) for a in a_list]
    in_specs.append(_w_spec(w, layer, (k_total, tn), lambda i, j: (0, j + off_blocks)))
    in_specs += [pl.BlockSpec(bs, im) for (_, bs, im) in extras]
    out_specs = [pl.BlockSpec(bs, im) for (_, _, bs, im) in outs]
    out_shape = [jax.ShapeDtypeStruct(s, dt) for (s, dt, _, _) in outs]
    res = pl.pallas_call(
        functools.partial(_mm_kernel, n_a=len(a_list), n_extra=len(extras), epilogue=epilogue),
        grid=(m // tm, n_cols // tn),
        in_specs=in_specs,
        out_specs=out_specs,
        out_shape=out_shape,
        compiler_params=_cparams("parallel", "arbitrary"),
        name=name,
    )(*a_list, w, *[e[0] for e in extras])
    return res


def _epi_scale(acc, extra_refs, out_refs, *, scale):
    out_refs[0][...] = (acc * scale if scale != 1.0 else acc).astype(out_refs[0].dtype)


def _epi_silu(acc, extra_refs, out_refs):
    out_refs[0][...] = _silu(acc).astype(out_refs[0].dtype)


def _epi_residual(acc, extra_refs, out_refs):
    out_refs[0][...] = extra_refs[0][...] + acc


def _epi_rotary(acc, extra_refs, out_refs):
    cos = extra_refs[0][...]
    sin = extra_refs[1][...]
    half = RET_DIM // 2
    tn = acc.shape[1]
    is_q = pl.program_id(1) * tn < RET_WIDTH
    scale = jnp.where(is_q, 1.0, RET_DIM ** -0.5).astype(F32)
    for hd in range(tn // RET_DIM):
        lo = hd * RET_DIM
        x1 = acc[:, lo:lo + half]
        x2 = acc[:, lo + half:lo + RET_DIM]
        out_refs[0][:, lo:lo + half] = ((x1 * cos - x2 * sin) * scale).astype(BF16)
        out_refs[0][:, lo + half:lo + RET_DIM] = ((x1 * sin + x2 * cos) * scale).astype(BF16)


def _epi_segments(acc, extra_refs, out_refs, *, segments):
    j = pl.program_id(1)
    for lo, hi, epi in segments:
        pl.when((j >= lo) & (j < hi))(functools.partial(epi, acc, extra_refs, out_refs))


def _epi_od_proj(acc, extra_refs, out_refs):
    is_z = pl.program_id(1) * acc.shape[1] < SSM_INNER
    out_refs[0][...] = jnp.where(is_z, _silu(acc), acc).astype(BF16)


def _epi_log_sigmoid(acc, extra_refs, out_refs):
    z = acc + extra_refs[0][...]
    out_refs[0][...] = jnp.minimum(z, 0.0) - jnp.log1p(jnp.exp(-jnp.abs(z)))


def _epi_softplus(acc, extra_refs, out_refs):
    out_refs[0][...] = _softplus(acc + extra_refs[0][...])


def _simple_out(m, n, tm, tn, dtype):
    tm = min(tm, m)
    return [((m, n), dtype, (tm, tn), lambda i, j: (i, j))]


def _gateup_kernel(h_ref, wg_ref, wu_ref, o_ref):
    h = h_ref[...]
    g = _dot(h, wg_ref[...])
    u = _dot(h, wu_ref[...])
    o_ref[...] = (_silu(g) * u).astype(o_ref.dtype)


def gateup(h, wg, wu, layer, tn=512):
    m, k = h.shape
    n = wg.shape[-1]
    tm = min(TM, m)
    return pl.pallas_call(
        _gateup_kernel,
        grid=(m // tm, n // tn),
        in_specs=[pl.BlockSpec((tm, k), lambda i, j: (i, 0)),
                  _w_spec(wg, layer, (k, tn), lambda i, j: (0, j)),
                  _w_spec(wu, layer, (k, tn), lambda i, j: (0, j))],
        out_specs=pl.BlockSpec((tm, tn), lambda i, j: (i, j)),
        out_shape=jax.ShapeDtypeStruct((m, n), BF16),
        compiler_params=_cparams("parallel", "arbitrary"),
        name="ffn_gateup",
    )(h, wg, wu)


def _mm_k_residual_kernel(a_ref, w_ref, x_ref, o_ref, acc_ref):
    kk = pl.program_id(2)

    @pl.when((pl.program_id(0) == 0) & (pl.program_id(1) == 0) & (kk == 0))
    def _():
        acc_ref[...] = jnp.zeros_like(acc_ref)

    acc = jnp.where(kk == 0, 0.0, acc_ref[...]) + _dot(a_ref[...], w_ref[...])
    acc_ref[...] = acc
    o_ref[...] = x_ref[...] + acc


def matmul_k_residual(a, w, x, *, tk, tn=1024, name, layer=0):
    m, k = a.shape
    n = w.shape[-1]
    tm = min(TM, m)
    assert w.shape[-2] == k and k % tk == 0 and n % tn == 0
    return pl.pallas_call(
        _mm_k_residual_kernel,
        grid=(m // tm, n // tn, k // tk),
        in_specs=[pl.BlockSpec((tm, tk), lambda i, j, kk: (i, kk)),
                  _w_spec(w, layer, (tk, tn), lambda i, j, kk: (kk, j)),
                  pl.BlockSpec((tm, tn), lambda i, j, kk: (i, j))],
        out_specs=pl.BlockSpec((tm, tn), lambda i, j, kk: (i, j)),
        out_shape=jax.ShapeDtypeStruct((m, n), F32),
        scratch_shapes=[pltpu.VMEM((tm, tn), F32)],
        compiler_params=_cparams("parallel", "parallel", "arbitrary"),
        name=name,
    )(a, w, x)


def _cumsum_kernel(x_ref, o_ref, carry_ref):
    @pl.when(pl.program_id(0) == 0)
    def _():
        carry_ref[...] = jnp.zeros_like(carry_ref)

    n = x_ref.shape[0]
    tri = _lower_tri(n).astype(F32)
    c = _dot_f32(tri, x_ref[...]) + carry_ref[0:1, :]
    o_ref[...] = c
    carry_ref[...] = jnp.broadcast_to(c[n - 1:n, :], carry_ref.shape)


def cumsum_rows(x):
    m, n = x.shape
    tb = min(CUMSUM_BLOCK, m)
    return pl.pallas_call(
        _cumsum_kernel,
        grid=(m // tb,),
        in_specs=[pl.BlockSpec((tb, n), lambda i: (i, 0))],
        out_specs=pl.BlockSpec((tb, n), lambda i: (i, 0)),
        out_shape=jax.ShapeDtypeStruct((m, n), F32),
        scratch_shapes=[pltpu.VMEM((SUBLANES, n), F32)],
        compiler_params=_cparams("arbitrary"),
        name="cumsum_rows",
    )(x)


def _retention_kernel(lg_ref, q_ref, k_ref, v_ref, g_ref, o_ref, state_ref, *, chunk, n_sub):
    head = pl.program_id(0)

    @pl.when(pl.program_id(1) == 0)
    def _():
        state_ref[...] = jnp.zeros_like(state_ref)

    lg = lg_ref[head]
    L = chunk
    ti = lax.broadcasted_iota(jnp.int32, (L, L), 0)
    si = lax.broadcasted_iota(jnp.int32, (L, L), 1)
    causal = ti >= si
    diff = jnp.where(causal, (ti - si).astype(F32), 0.0)
    decay_in = jnp.where(causal, jnp.exp(lg * diff), 0.0)
    idx = lax.broadcasted_iota(jnp.int32, (L, 1), 0).astype(F32)
    q_decay = jnp.exp(lg * (idx + 1.0))
    k_decay = jnp.exp(lg * (L - 1.0 - idx))
    chunk_decay = jnp.exp(lg * float(L))

    for u in range(n_sub):
        rows = slice(u * L, (u + 1) * L)
        q = q_ref[rows, :]
        k = k_ref[rows, :]
        v = v_ref[rows, :]
        state = state_ref[...]
        scores = _dot_nt(q, k) * decay_in
        inner = _dot(scores.astype(BF16), v)
        cross = _dot(q, state.astype(BF16)) * q_decay
        out = inner + cross
        kd = (k.astype(F32) * k_decay).astype(BF16)
        state_ref[...] = state * chunk_decay + _dot_tn(kd, v)
        ms = jnp.mean(out * out, axis=-1, keepdims=True)
        o_ref[rows, :] = (out * lax.rsqrt(ms + EPS) * g_ref[rows, :].astype(F32)).astype(o_ref.dtype)


def retention(proj, log_gamma):
    s = proj.shape[0]
    chunk = min(RET_CHUNK, s)
    tb = min(RET_BLOCK, s)
    return pl.pallas_call(
        functools.partial(_retention_kernel, chunk=chunk, n_sub=tb // chunk),
        grid=(RET_HEADS, s // tb),
        in_specs=[pl.BlockSpec(memory_space=pltpu.SMEM),
                  pl.BlockSpec((tb, RET_DIM), lambda h, c: (c, h)),
                  pl.BlockSpec((tb, RET_DIM), lambda h, c: (c, RET_HEADS + h)),
                  pl.BlockSpec((tb, RET_DIM), lambda h, c: (c, 2 * RET_HEADS + h)),
                  pl.BlockSpec((tb, RET_DIM), lambda h, c: (c, 3 * RET_HEADS + h))],
        out_specs=pl.BlockSpec((tb, RET_DIM), lambda h, c: (c, h)),
        out_shape=jax.ShapeDtypeStruct((s, RET_WIDTH), BF16),
        scratch_shapes=[pltpu.VMEM((RET_DIM, RET_DIM), F32)],
        compiler_params=_cparams("parallel", "arbitrary"),
        name="retention",
    )(log_gamma, proj, proj, proj, proj)


def _split3(x):
    hi = x.astype(BF16)
    r1 = x - hi.astype(F32)
    mid = r1.astype(BF16)
    lo = (r1 - mid.astype(F32)).astype(BF16)
    return hi, mid, lo


def _fox_prep_kernel(q_ref, k_ref, v_ref, c_ref, qa_ref, ka_ref, va_ref):
    tb = q_ref.shape[0]
    d = FOX_HEAD_DIM
    lane = lax.broadcasted_iota(jnp.int32, (tb, d), 1)
    one = jnp.ones((tb, d), F32)
    zero = jnp.zeros((tb, d), F32)
    c_all = c_ref[...] * LOG2E
    for hd in range(FOX_HEADS):
        hi, mid, lo = [jnp.broadcast_to(part.astype(F32), (tb, d)) for part in _split3(c_all[:, hd:hd + 1])]
        aug_q = jnp.where(lane == 0, hi, jnp.where(lane == 1, mid, jnp.where(lane == 2, lo,
                          jnp.where(lane < 6, one, zero))))
        aug_k = jnp.where(lane < 3, one, jnp.where(lane == 3, -hi, jnp.where(lane == 4, -mid,
                          jnp.where(lane == 5, -lo, zero))))
        src = slice(hd * d, (hd + 1) * d)
        data = slice(2 * hd * d, (2 * hd + 1) * d)
        aug = slice((2 * hd + 1) * d, (2 * hd + 2) * d)
        qa_ref[:, data] = q_ref[:, src]
        qa_ref[:, aug] = aug_q.astype(BF16)
        ka_ref[:, data] = k_ref[:, src]
        ka_ref[:, aug] = aug_k.astype(BF16)
        va_ref[:, data] = v_ref[:, src]
        va_ref[:, aug] = one.astype(BF16)


def fox_prep(proj, c, tb=512):
    s = proj.shape[0]
    tb = min(tb, s)
    w = FOX_WIDTH
    first = 4 * RET_WIDTH // w
    out = jax.ShapeDtypeStruct((s, 2 * w), BF16)
    return pl.pallas_call(
        _fox_prep_kernel,
        grid=(s // tb,),
        in_specs=[pl.BlockSpec((tb, w), lambda i: (i, first)),
                  pl.BlockSpec((tb, w), lambda i: (i, first + 1)),
                  pl.BlockSpec((tb, w), lambda i: (i, first + 2)),
                  pl.BlockSpec((tb, LANES), lambda i: (i, 0))],
        out_specs=[pl.BlockSpec((tb, 2 * w), lambda i: (i, 0))] * 3,
        out_shape=[out, out, out],
        compiler_params=_cparams("parallel"),
        name="fox_prep",
    )(proj, proj, proj, c)


def _fox_kernel(q_ref, k_ref, v_ref, o_ref, s0_ref, s1_ref, m_ref, acc_ref, *, blk, sub):
    qi = pl.program_id(1)
    d = FOX_HEAD_DIM
    q = q_ref[...]

    def scores(jsub):
        start = pl.multiple_of(jsub * sub, sub)
        return _dot_nt(q, k_ref[pl.ds(start, sub), :])

    def process(s_ref, jsub, mask_off):
        start = pl.multiple_of(jsub * sub, sub)
        s = s_ref[...]
        if mask_off is not None:
            r = lax.broadcasted_iota(jnp.int32, (blk, sub), 0)
            c = lax.broadcasted_iota(jnp.int32, (blk, sub), 1)
            s = jnp.where(r >= c + mask_off, s, NEG_BIG)
        m_old = m_ref[...]
        m_new = jnp.maximum(m_old, jnp.max(s, axis=1, keepdims=True))
        alpha = jnp.exp2(m_old - m_new)
        p = jnp.concatenate([jnp.exp2(s[:, u * d:(u + 1) * d] - m_new) for u in range(sub // d)], axis=1)
        pv = _dot(p.astype(BF16), v_ref[pl.ds(start, sub), :])
        acc_ref[:, :d] = alpha * acc_ref[:, :d] + pv[:, :d]
        acc_ref[:, d:] = alpha * acc_ref[:, d:] + pv[:, d:]
        m_ref[...] = m_new

    m_ref[...] = jnp.full_like(m_ref, NEG_BIG)
    acc_ref[...] = jnp.zeros_like(acc_ref)
    s0_ref[...] = scores(0)

    def body(t, carry):
        s1_ref[...] = scores(2 * t + 1)
        process(s0_ref, 2 * t, None)
        s0_ref[...] = scores(2 * t + 2)
        process(s1_ref, 2 * t + 1, None)
        return carry

    lax.fori_loop(0, qi, body, 0)
    s1_ref[...] = scores(2 * qi + 1)
    process(s0_ref, 2 * qi, 0)
    process(s1_ref, 2 * qi + 1, sub)
    o_ref[...] = (acc_ref[:, :d] / acc_ref[:, d:]).astype(o_ref.dtype)


def fox_attention(qa, ka, va):
    s = qa.shape[0]
    blk = min(FOX_BLOCK, s)
    sub = blk // 2
    d = FOX_HEAD_DIM
    return pl.pallas_call(
        functools.partial(_fox_kernel, blk=blk, sub=sub),
        grid=(FOX_HEADS, s // blk),
        in_specs=[pl.BlockSpec((blk, 2 * d), lambda h, i: (i, h)),
                  pl.BlockSpec((s, 2 * d), lambda h, i: (0, h)),
                  pl.BlockSpec((s, 2 * d), lambda h, i: (0, h))],
        out_specs=pl.BlockSpec((blk, d), lambda h, i: (i, h)),
        out_shape=jax.ShapeDtypeStruct((s, FOX_WIDTH), BF16),
        scratch_shapes=[pltpu.VMEM((blk, sub), F32), pltpu.VMEM((blk, sub), F32),
                        pltpu.VMEM((blk, d), F32), pltpu.VMEM((blk, 2 * d), F32)],
        compiler_params=_cparams("parallel", "arbitrary"),
        name="fox_attention",
    )(qa, ka, va)


def _rms_scale(x, g):
    return x * lax.rsqrt(jnp.mean(x * x, axis=-1, keepdims=True) + EPS) * g


def _xattn_kernel(x_ref, g_in_ref, g_out_ref, wq_ref, kt_ref, v_ref, wo_ref, o_ref, h_out_ref):
    x = x_ref[...]
    h = _rms_scale(x, g_in_ref[...]).astype(BF16)
    q = _dot(h, wq_ref[...]).astype(BF16)
    heads = []
    for hd in range(XA_HEADS):
        cols = slice(hd * XA_HEAD_DIM, (hd + 1) * XA_HEAD_DIM)
        logits = _dot(q[:, cols], kt_ref[hd]) * (XA_HEAD_DIM ** -0.5)
        mx = jnp.max(logits, axis=-1, keepdims=True)
        e = jnp.exp(logits - mx)
        p = e / jnp.sum(e, axis=-1, keepdims=True)
        heads.append(_dot(p.astype(BF16), v_ref[:, cols]))
    o = jnp.concatenate(heads, axis=-1).astype(BF16)
    x_new = x + _dot(o, wo_ref[...])
    o_ref[...] = x_new
    h_out_ref[...] = _rms_scale(x_new, g_out_ref[...]).astype(BF16)


def cross_attention(x, g_in, g_out, wq, k_t, v, wo, layer, tm=512):
    s, d = x.shape
    tm = min(tm, s)
    mem_len = v.shape[0]
    const2 = lambda i: (0, 0)
    row = lambda i: (i, 0)
    once = pl.Buffered(1)
    return pl.pallas_call(
        _xattn_kernel,
        grid=(s // tm,),
        in_specs=[pl.BlockSpec((tm, d), row),
                  pl.BlockSpec((1, d), const2),
                  pl.BlockSpec((1, d), const2),
                  pl.BlockSpec((None, d, XA_WIDTH), lambda i: (layer, 0, 0), pipeline_mode=once),
                  pl.BlockSpec((XA_HEADS, XA_HEAD_DIM, mem_len), lambda i: (0, 0, 0)),
                  pl.BlockSpec((mem_len, XA_WIDTH), const2),
                  pl.BlockSpec((None, XA_WIDTH, d), lambda i: (layer, 0, 0), pipeline_mode=once)],
        out_specs=[pl.BlockSpec((tm, d), row), pl.BlockSpec((tm, d), row)],
        out_shape=[jax.ShapeDtypeStruct((s, d), F32), jax.ShapeDtypeStruct((s, d), BF16)],
        compiler_params=_cparams("parallel"),
        name="cross_attention",
    )(x, g_in.astype(F32).reshape(1, d), g_out.astype(F32).reshape(1, d), wq, k_t, v, wo)


SSD_HALO = 128
SSD_REP = 3
SSD_REP_LANES = LANES
SSD_GROUPS_PER_STEP = 1


def _ssd_constants(L):
    rows = jnp.arange((SSM_CONV - 1) * L)
    src = SSD_HALO + rows % L - (rows // L + 1)
    shift = (jnp.arange(L + SSD_HALO)[None, :] == src[:, None]).astype(BF16)
    lane = jnp.arange(SSD_REP_LANES)
    part_head = jnp.where(lane < SSD_REP * SSM_HPG, lane % SSM_HPG, -1)
    e_col = (part_head[:, None] == (jnp.arange(SSM_HPG * L) // L)[None, :]).astype(BF16)
    e_head = (part_head[:, None] == (jnp.arange(SSM_GROUP_WIDTH) // SSM_HEAD_DIM)[None, :]).astype(BF16)
    return shift, e_col, e_head


def _replicate_heads(x):
    x3 = jnp.tile(x, (1,) * (x.ndim - 1) + (SSD_REP,))
    return jnp.pad(x3, [(0, 0)] * (x.ndim - 1) + [(0, SSD_REP_LANES - SSD_REP * SSM_HPG)])


def _pack_split3(x3):
    lane = lax.broadcasted_iota(jnp.int32, x3.shape, 1)
    hi = x3.astype(BF16).astype(F32)
    r1 = x3 - hi
    mid = r1.astype(BF16).astype(F32)
    lo = r1 - mid
    parts = jnp.where(lane < SSM_HPG, hi, jnp.where(lane < 2 * SSM_HPG, mid, lo))
    return jnp.where(lane < SSD_REP * SSM_HPG, parts, 0.0).astype(BF16)


def _ssd_kernel(x_ref, b_ref, c_ref, wx_ref, wb_ref, wc_ref, bx_ref, bb_ref, bc_ref,
                shift_ref, ecol_ref, ehead_ref, dt_ref, dtt_ref, a_ref, at_ref, dskip_ref, z_ref, gnw_ref,
                y_ref, state_ref, *halo_refs, L):
    gw = SSM_GROUP_WIDTH
    n = SSM_STATE

    @pl.when(pl.program_id(1) == 0)
    def _():
        state_ref[...] = jnp.zeros_like(state_ref)
        for halo_ref in halo_refs:
            halo_ref[...] = jnp.zeros_like(halo_ref)

    causal = _lower_tri(L)
    tri = causal.astype(F32)
    first_half = lax.broadcasted_iota(jnp.int32, (L, LANES), 1) < SSM_HEAD_DIM

    for u in range(SSD_GROUPS_PER_STEP):
        wide_cols = slice(u * gw, (u + 1) * gw)
        state_cols = slice(u * n, (u + 1) * n)

        cur16 = jnp.concatenate([x_ref[:, wide_cols], b_ref[:, state_cols], c_ref[:, state_cols]], axis=1)
        buf = jnp.concatenate([halo_refs[u][...], cur16], axis=0)
        shifted = _dot(shift_ref[...], buf)
        cur = cur16.astype(F32)
        halo_refs[u][...] = cur16[L - SSD_HALO:, :]

        def conv_silu(cols, w, bias):
            y = bias + w[SSM_CONV - 1:SSM_CONV, :] * cur[:, cols]
            for back in range(1, SSM_CONV):
                y = y + w[SSM_CONV - 1 - back:SSM_CONV - back, :] * shifted[(back - 1) * L:back * L, cols]
            return _silu(y)

        xs = conv_silu(slice(0, gw), wx_ref[:, wide_cols], bx_ref[:, wide_cols])
        bm16 = conv_silu(slice(gw, gw + n), wb_ref[:, state_cols], bb_ref[:, state_cols]).astype(BF16)
        cm16 = conv_silu(slice(gw + n, gw + 2 * n), wc_ref[:, state_cols], bc_ref[:, state_cols]).astype(BF16)

        dt3 = dt_ref[u]
        acs3 = _dot_f32(tri, dt3 * a_ref[u])
        acs_t = lax.dot_general(dtt_ref[u] * at_ref[u], tri, (((1,), (1,)), ((), ())),
                                preferred_element_type=F32, precision=lax.Precision.HIGHEST)
        acs_last = acs3[L - 1:L, :]
        lhs = jnp.concatenate([_pack_split3(dt3), _pack_split3(jnp.exp2(acs3)),
                               _pack_split3(dt3 * jnp.exp2(acs_last - acs3))], axis=0)
        wide = _dot(lhs, ehead_ref[...])
        dt_w = wide[0:L, :]
        exp_acs_w = wide[L:2 * L, :]
        dt_tail_w = wide[2 * L:, :]
        acs_col = _dot(_pack_split3(acs3), ecol_ref[...])

        cb = _dot_nt(cm16, bm16)
        state = state_ref[u]
        y_off = _dot(cm16, state.astype(BF16))

        xdt16 = (xs * dt_w).astype(BF16)
        y_parts = []
        for pr in range(SSM_HPG // 2):
            e0, e1 = 2 * pr, 2 * pr + 1
            m0 = jnp.exp2(jnp.where(causal, acs_col[:, e0 * L:(e0 + 1) * L] - acs_t[e0:e0 + 1, :], NEG_BIG)) * cb
            m1 = jnp.exp2(jnp.where(causal, acs_col[:, e1 * L:(e1 + 1) * L] - acs_t[e1:e1 + 1, :], NEG_BIG)) * cb
            mcat = jnp.concatenate([m0, m1], axis=1).astype(BF16)
            pair16 = xdt16[:, pr * LANES:(pr + 1) * LANES]
            zero = jnp.zeros_like(pair16)
            bd = jnp.concatenate([jnp.where(first_half, pair16, zero),
                                  jnp.where(first_half, zero, pair16)], axis=0)
            y_parts.append(_dot(mcat, bd))
        y = jnp.concatenate(y_parts, axis=1) + y_off * exp_acs_w
        state_ref[u] = state * exp_acs_w[L - 1:L, :] + _dot_tn(bm16, (xs * dt_tail_w).astype(BF16))

        y = (y + xs * dskip_ref[:, wide_cols]) * z_ref[:, wide_cols].astype(F32)
        ms = jnp.mean(y * y, axis=-1, keepdims=True)
        y_ref[:, wide_cols] = (y * lax.rsqrt(ms + EPS) * gnw_ref[:, wide_cols]).astype(y_ref.dtype)


def ssd_mixer(proj, conv_w, conv_b, dt_g, dt_gt, a_g, a_gt, dskip_row, gn_w):
    s = proj.shape[0]
    L = min(SSD_CHUNK, s)
    p = SSD_GROUPS_PER_STEP
    gw = p * SSM_GROUP_WIDTH
    n = p * SSM_STATE
    xb = SSM_INNER // n
    cbk = xb + SSM_GROUPS // p
    pz = SSM_INNER // gw
    pb = SSM_INNER // n
    rep = SSD_REP_LANES
    shift, e_col, e_head = _ssd_constants(L)
    return pl.pallas_call(
        functools.partial(_ssd_kernel, L=L),
        grid=(SSM_GROUPS // p, s // L),
        in_specs=[pl.BlockSpec((L, gw), lambda g, c: (c, pz + g)),
                  pl.BlockSpec((L, n), lambda g, c: (c, pb + xb + g)),
                  pl.BlockSpec((L, n), lambda g, c: (c, pb + cbk + g)),
                  pl.BlockSpec((SSM_CONV, gw), lambda g, c: (0, g)),
                  pl.BlockSpec((SSM_CONV, n), lambda g, c: (0, xb + g)),
                  pl.BlockSpec((SSM_CONV, n), lambda g, c: (0, cbk + g)),
                  pl.BlockSpec((1, gw), lambda g, c: (0, g)),
                  pl.BlockSpec((1, n), lambda g, c: (0, xb + g)),
                  pl.BlockSpec((1, n), lambda g, c: (0, cbk + g)),
                  pl.BlockSpec(shift.shape, lambda g, c: (0, 0)),
                  pl.BlockSpec(e_col.shape, lambda g, c: (0, 0)),
                  pl.BlockSpec(e_head.shape, lambda g, c: (0, 0)),
                  pl.BlockSpec((p, L, rep), lambda g, c: (g, c, 0)),
                  pl.BlockSpec((p, SSM_HPG, L), lambda g, c: (g, 0, c)),
                  pl.BlockSpec((p, 1, rep), lambda g, c: (g, 0, 0)),
                  pl.BlockSpec((p, SSM_HPG, 1), lambda g, c: (g, 0, 0)),
                  pl.BlockSpec((1, gw), lambda g, c: (0, g)),
                  pl.BlockSpec((L, gw), lambda g, c: (c, g)),
                  pl.BlockSpec((1, gw), lambda g, c: (0, g))],
        out_specs=pl.BlockSpec((L, gw), lambda g, c: (c, g)),
        out_shape=jax.ShapeDtypeStruct((s, SSM_INNER), BF16),
        scratch_shapes=[pltpu.VMEM((p, SSM_STATE, SSM_GROUP_WIDTH), F32)]
        + [pltpu.VMEM((SSD_HALO, SSM_GROUP_WIDTH + 2 * SSM_STATE), BF16)] * p,
        compiler_params=_cparams("parallel", "arbitrary"),
        name="ssd_mixer",
    )(proj, proj, proj, conv_w, conv_w, conv_w, conv_b, conv_b, conv_b, shift, e_col, e_head,
      _replicate_heads(dt_g), dt_gt, _replicate_heads(a_g), a_gt, dskip_row, proj, gn_w)


PROJ_TN = 512
PROJ_TM = 2048
EV_PROJ_COLS = 4 * RET_WIDTH + 3 * FOX_WIDTH
OD_PROJ_COLS = SSM_INNER + SSM_CONV_DIM


def _retention_fox_layer(x, norm_g, w_in16, i, b_f, w_out16):
    s = x.shape[0]
    tm = min(TM, s)
    h = rmsnorm(x, norm_g, BF16)

    half = RET_DIM // 2
    inv = ROPE_BASE ** (-jnp.arange(half, dtype=F32) / half)
    ang = jnp.arange(s).astype(F32)[:, None] * inv[None, :]
    cos, sin = jnp.cos(ang), jnp.sin(ang)

    plain = functools.partial(_epi_scale, scale=1.0)
    fq_scale = functools.partial(_epi_scale, scale=LOG2E * FOX_HEAD_DIM ** -0.5)
    bounds = [0, 2 * RET_WIDTH, 3 * RET_WIDTH, 4 * RET_WIDTH, 4 * RET_WIDTH + FOX_WIDTH, EV_PROJ_COLS]
    epis = [_epi_rotary, plain, _epi_silu, fq_scale, plain]
    segments = [(lo // PROJ_TN, hi // PROJ_TN, e) for lo, hi, e in zip(bounds[:-1], bounds[1:], epis)]
    tmp = min(PROJ_TM, s)
    (proj,) = matmul([h], w_in16, layer=i, col_off=0, n_cols=EV_PROJ_COLS, tn=PROJ_TN, tm=tmp,
                     epilogue=functools.partial(_epi_segments, segments=segments),
                     extras=[(cos, (tmp, half), lambda i_, j: (i_, 0)), (sin, (tmp, half), lambda i_, j: (i_, 0))],
                     outs=_simple_out(s, EV_PROJ_COLS, tmp, PROJ_TN, BF16), name="ev_proj")

    w_ff = jnp.pad(w_in16[i, :, EV_PROJ_COLS:EV_PROJ_COLS + FOX_HEADS], ((0, 0), (0, LANES - FOX_HEADS)))
    b_pad = jnp.pad(b_f.astype(F32), (0, LANES - FOX_HEADS)).reshape(1, LANES)
    (log_f,) = matmul([h], w_ff, col_off=0, n_cols=LANES, tn=LANES, epilogue=_epi_log_sigmoid,
                      extras=[(b_pad, (1, LANES), lambda i_, j: (0, 0))],
                      outs=_simple_out(s, LANES, tm, LANES, F32), name="ev_proj_forget")
    c = cumsum_rows(log_f)

    log_gamma = jnp.log1p(-jnp.exp2(-5.0 - jnp.arange(RET_HEADS, dtype=F32)))
    ret = retention(proj, log_gamma)
    fox = fox_attention(*fox_prep(proj, c))

    (x_new,) = matmul([ret, fox], w_out16, layer=i, col_off=0, n_cols=D_MODEL, tn=1024, epilogue=_epi_residual,
                      extras=[(x, (tm, 1024), lambda i_, j: (i_, j))],
                      outs=_simple_out(s, D_MODEL, tm, 1024, F32), name="ev_out_proj")
    return x_new


def _mamba2_layer(x, norm_g, w_in16, i, conv_w, conv_b, dt_bias, a_log, d_skip, gn_w, w_out16):
    s = x.shape[0]
    tm = min(TM, s)
    h = rmsnorm(x, norm_g, BF16)
    tmp = min(PROJ_TM, s)
    (proj,) = matmul([h], w_in16, layer=i, col_off=0, n_cols=OD_PROJ_COLS, tn=PROJ_TN, tm=tmp, epilogue=_epi_od_proj,
                     outs=_simple_out(s, OD_PROJ_COLS, tmp, PROJ_TN, BF16), name="od_proj")
    (dt,) = matmul([h], w_in16, layer=i, col_off=OD_PROJ_COLS, n_cols=SSM_HEADS, tn=SSM_HEADS,
                   epilogue=_epi_softplus,
                   extras=[(dt_bias.astype(F32).reshape(1, SSM_HEADS), (1, SSM_HEADS), lambda i_, j: (0, 0))],
                   outs=_simple_out(s, SSM_HEADS, tm, SSM_HEADS, F32), name="od_proj_dt")

    dt_g = dt.reshape(s, SSM_GROUPS, SSM_HPG).transpose(1, 0, 2)
    dt_gt = dt_g.transpose(0, 2, 1)
    a = -jnp.exp(a_log.astype(F32)) * LOG2E
    a_g = a.reshape(SSM_GROUPS, 1, SSM_HPG)
    a_gt = a.reshape(SSM_GROUPS, SSM_HPG, 1)
    dskip_row = jnp.repeat(d_skip.astype(F32), SSM_HEAD_DIM).reshape(1, SSM_INNER)
    y = ssd_mixer(proj, conv_w.astype(F32), conv_b.astype(F32).reshape(1, SSM_CONV_DIM),
                  dt_g, dt_gt, a_g, a_gt, dskip_row, gn_w.astype(F32).reshape(1, SSM_INNER))
    tn = 256
    (x_new,) = matmul([y], w_out16, layer=i, col_off=0, n_cols=D_MODEL, tn=tn, epilogue=_epi_residual,
                      extras=[(x, (tm, tn), lambda i_, j: (i_, j))],
                      outs=_simple_out(s, D_MODEL, tm, tn, F32), name="od_out_proj")
    return x_new


def _cross_attention_layer(x, g_in, g_out, mem_n, wq16, wk16, wv16, wo16, layer):
    m = mem_n.shape[0]
    plain = functools.partial(_epi_scale, scale=1.0)
    (k,) = matmul([mem_n], wk16, layer=layer, col_off=0, n_cols=XA_WIDTH, tn=XA_WIDTH, epilogue=plain,
                  outs=_simple_out(m, XA_WIDTH, m, XA_WIDTH, BF16), name="xa_proj_k")
    (v,) = matmul([mem_n], wv16, layer=layer, col_off=0, n_cols=XA_WIDTH, tn=XA_WIDTH, epilogue=plain,
                  outs=_simple_out(m, XA_WIDTH, m, XA_WIDTH, BF16), name="xa_proj_v")
    k_t = k.reshape(m, XA_HEADS, XA_HEAD_DIM).transpose(1, 2, 0)
    return cross_attention(x, g_in, g_out, wq16, k_t, v, wo16, layer)


def kernel(x, mem, mem_norm, ev_mix_norm, ev_w_in, ev_b_f, ev_w_out, od_mix_norm, od_w_in, od_conv_w, od_conv_b,
           od_dt_bias, od_a_log, od_d_skip, od_gn_w, od_w_out, xa_norm, xa_wq, xa_wk, xa_wv, xa_wo, ffn_norm,
           ffn_w_gate, ffn_w_up, ffn_w_down, final_norm):
    b, s, d = x.shape
    assert b == 1 and d == D_MODEL
    xs = x.reshape(s, d).astype(F32)
    mem_n = rmsnorm(mem.reshape(mem.shape[1], d).astype(F32), mem_norm, BF16)

    ev_w_in16, ev_w_out16 = ev_w_in.astype(BF16), ev_w_out.astype(BF16)
    od_w_in16, od_w_out16 = od_w_in.astype(BF16), od_w_out.astype(BF16)
    xa_wq16, xa_wk16, xa_wv16, xa_wo16 = [w.astype(BF16) for w in (xa_wq, xa_wk, xa_wv, xa_wo)]
    wg16 = cast_pad_bf16(ffn_w_gate, cols_out=D_FF_PAD)
    wu16 = cast_pad_bf16(ffn_w_up, cols_out=D_FF_PAD)
    wd16 = cast_pad_bf16(ffn_w_down, rows_out=D_FF_PAD)

    depth = xa_norm.shape[0]
    for layer in range(depth):
        i = layer // 2
        if layer % 2 == 0:
            xs = _retention_fox_layer(xs, ev_mix_norm[i], ev_w_in16, i, ev_b_f[i], ev_w_out16)
        else:
            xs = _mamba2_layer(xs, od_mix_norm[i], od_w_in16, i, od_conv_w[i], od_conv_b[i], od_dt_bias[i],
                               od_a_log[i], od_d_skip[i], od_gn_w[i], od_w_out16)
        xs, h_ffn = _cross_attention_layer(xs, xa_norm[layer], ffn_norm[layer], mem_n,
                                           xa_wq16, xa_wk16, xa_wv16, xa_wo16, layer)
        a = gateup(h_ffn, wg16, wu16, layer)
        xs = matmul_k_residual(a, wd16, xs, layer=layer, tk=DOWN_TK, name="ffn_down")
    out = rmsnorm(xs, final_norm, F32)
    return out.reshape(b, s, d)
```
